```python
import jax, jax.numpy as jnp
from jax import lax
import numpy as np

D_MODEL = 2048
BATCH = 4
SEQ = 4096
DEPTH = 1

N_HEADS = 16
N_KV_HEADS = 4
HEAD_DIM = 128
Q_PER_KV = N_HEADS // N_KV_HEADS
Q_BLOCK = 128
ROPE_THETA = 10000.0
GRID_W = 64
ATTN_Q_DIM = N_HEADS * HEAD_DIM
ATTN_KV_DIM = N_KV_HEADS * HEAD_DIM
N_FOURIER_GROUPS = 4
FOURIER_GROUP_DIM = 256
FOURIER_DIM = N_FOURIER_GROUPS * FOURIER_GROUP_DIM
IN_PROJ_DIM = ATTN_Q_DIM + 2 * ATTN_KV_DIM + FOURIER_DIM + 2 * D_MODEL
N_GROUPS = 4
EXPERTS_PER_GROUP = 8
N_EXPERTS = N_GROUPS * EXPERTS_PER_GROUP
TOP_K_IN_GROUP = 2
EXPERT_FF = 512
NORM_EPS = 1e-6

kernel_name = "hybrid_gqa_fnet_hmoe_block"


def rmsnorm(x, g):
    xf = x.astype(jnp.float32)
    y = xf * lax.rsqrt(jnp.mean(xf * xf, axis=-1, keepdims=True) + NORM_EPS)
    return (y * g.astype(jnp.float32)).astype(x.dtype)


def axial_rope_tables(seq):
    rows = seq // GRID_W
    row = jnp.repeat(jnp.arange(rows, dtype=jnp.float32), GRID_W)
    col = jnp.tile(jnp.arange(GRID_W, dtype=jnp.float32), rows)
    n_freq = HEAD_DIM // 4
    inv = ROPE_THETA ** (-jnp.arange(n_freq, dtype=jnp.float32) / n_freq)
    ang_r = row[:, None] * inv
    ang_c = col[:, None] * inv
    ang = jnp.concatenate([ang_r, ang_r, ang_c, ang_c], axis=-1)
    return jnp.cos(ang), jnp.sin(ang)


def apply_axial_rope(x, cos, sin):
    xs = x.reshape(x.shape[:-1] + (2, 2, HEAD_DIM // 4))
    rot = jnp.stack([-xs[..., 1, :], xs[..., 0, :]], axis=-2).reshape(x.shape)
    return (x.astype(jnp.float32) * cos + rot.astype(jnp.float32) * sin).astype(x.dtype)


def blocked_gqa(q, k, v):
    b, kvh, g, s, d = q.shape
    nb = s // Q_BLOCK
    scale = HEAD_DIM ** -0.5
    qb = jnp.moveaxis(q.reshape(b, kvh, g, nb, Q_BLOCK, d), 3, 0)

    def attend(qblk):
        sc = jnp.einsum('bkgqd,bksd->bkgqs', qblk, k).astype(jnp.float32) * scale
        p = jax.nn.softmax(sc, axis=-1).astype(v.dtype)
        return jnp.einsum('bkgqs,bksd->bkgqd', p, v)

    o = lax.map(attend, qb)
    o = jnp.transpose(o, (1, 0, 4, 2, 3, 5))
    return o.reshape(b, s, kvh * g * d)


def fourier_mix(f):
    b, s, _ = f.shape
    fg = f.astype(jnp.float32).reshape(b, s, N_FOURIER_GROUPS, FOURIER_GROUP_DIM)
    out = jnp.real(jnp.fft.fft2(fg, axes=(1, 3), norm="ortho"))
    return out.reshape(b, s, FOURIER_DIM).astype(f.dtype)


def hierarchical_moe(h, w_group_router, b_group_router, w_expert_router, b_expert_router,
                     w_exp_gate, w_exp_up, w_exp_down):
    b, s, d = h.shape
    ht = h.reshape(b * s, d)
    gl = (ht @ w_group_router + b_group_router).astype(jnp.float32)
    gp = jax.nn.softmax(gl, axis=-1)
    g_idx = jnp.argmax(gp, axis=-1)
    g_w = jnp.max(gp, axis=-1)
    el = (ht @ w_expert_router + b_expert_router).astype(jnp.float32)
    el = el.reshape(-1, N_GROUPS, EXPERTS_PER_GROUP)
    sel = jnp.take_along_axis(el, g_idx[:, None, None], axis=1)[:, 0]
    ep = jax.nn.softmax(sel, axis=-1)
    top_v, top_i = lax.top_k(ep, TOP_K_IN_GROUP)
    top_v = top_v / jnp.sum(top_v, axis=-1, keepdims=True)
    e_idx = g_idx[:, None] * EXPERTS_PER_GROUP + top_i
    combine = g_w[:, None] * jnp.sum(top_v[..., None] * jax.nn.one_hot(e_idx, N_EXPERTS, dtype=jnp.float32), axis=1)
    combine = combine.astype(h.dtype)
    out = jnp.zeros_like(ht)
    for e in range(N_EXPERTS):
        y = (jax.nn.silu(ht @ w_exp_gate[e]) * (ht @ w_exp_up[e])) @ w_exp_down[e]
        out = out + combine[:, e:e + 1] * y
    return out.reshape(b, s, d)


def setup_inputs(seed: int = 0) -> dict:
    key = jax.random.key(seed)
    ks = jax.random.split(key, 24)
    f32 = jnp.float32

    def w(k, shape, fan_in, mult=1.0):
        return jax.random.normal(k, shape, f32) * (mult * fan_in ** -0.5)

    def gain(k, shape):
        return 1.0 + 0.05 * jax.random.normal(k, shape, f32)

    return {
        "x": jax.random.normal(ks[0], (BATCH, SEQ, D_MODEL), f32),
        "c": jax.random.normal(ks[1], (BATCH, D_MODEL), f32),
        "w_ada": w(ks[2], (D_MODEL, 6 * D_MODEL), D_MODEL, 0.5),
        "b_ada": 0.02 * jax.random.normal(ks[3], (6 * D_MODEL,), f32),
        "g_pre_mix": gain(ks[4], (D_MODEL,)),
        "g_post_mix": gain(ks[5], (D_MODEL,)),
        "w_in": w(ks[6], (D_MODEL, IN_PROJ_DIM), D_MODEL),
        "q_norm_g": gain(ks[7], (HEAD_DIM,)),
        "k_norm_g": gain(ks[8], (HEAD_DIM,)),
        "w_attn_out": w(ks[9], (ATTN_Q_DIM, D_MODEL), ATTN_Q_DIM),
        "w_fourier_out": w(ks[10], (FOURIER_DIM, D_MODEL), FOURIER_DIM),
        "w_mix_out": w(ks[11], (D_MODEL, D_MODEL), D_MODEL),
        "g_pre_ffn": gain(ks[12], (D_MODEL,)),
        "g_post_ffn": gain(ks[13], (D_MODEL,)),
        "w_group_router": w(ks[14], (D_MODEL, N_GROUPS), D_MODEL),
        "b_group_router": 0.01 * jax.random.normal(ks[15], (N_GROUPS,), f32),
        "w_expert_router": w(ks[16], (D_MODEL, N_EXPERTS), D_MODEL),
        "b_expert_router": 0.01 * jax.random.normal(ks[17], (N_EXPERTS,), f32),
        "w_exp_gate": w(ks[18], (N_EXPERTS, D_MODEL, EXPERT_FF), D_MODEL),
        "w_exp_up": w(ks[19], (N_EXPERTS, D_MODEL, EXPERT_FF), D_MODEL),
        "w_exp_down": w(ks[20], (N_EXPERTS, EXPERT_FF, D_MODEL), EXPERT_FF),
    }


def reference(x, c, w_ada, b_ada, g_pre_mix, g_post_mix, w_in, q_norm_g, k_norm_g,
              w_attn_out, w_fourier_out, w_mix_out, g_pre_ffn, g_post_ffn,
              w_group_router, b_group_router, w_expert_router, b_expert_router,
              w_exp_gate, w_exp_up, w_exp_down):
    b, s, d = x.shape
    cos, sin = axial_rope_tables(s)

    for _ in range(DEPTH):
        mod = jax.nn.silu(c) @ w_ada + b_ada
        sh1, sc1, gt1, sh2, sc2, gt2 = [m[:, None, :] for m in jnp.split(mod, 6, axis=-1)]

        h = rmsnorm(x, g_pre_mix) * (1.0 + sc1) + sh1
        proj = h @ w_in
        o1 = ATTN_Q_DIM
        o2 = o1 + ATTN_KV_DIM
        o3 = o2 + ATTN_KV_DIM
        o4 = o3 + FOURIER_DIM
        o5 = o4 + D_MODEL
        q = proj[..., :o1].reshape(b, s, N_KV_HEADS, Q_PER_KV, HEAD_DIM)
        k = proj[..., o1:o2].reshape(b, s, N_KV_HEADS, HEAD_DIM)
        v = proj[..., o2:o3].reshape(b, s, N_KV_HEADS, HEAD_DIM)
        f_in = proj[..., o3:o4]
        gate_attn = jax.nn.sigmoid(proj[..., o4:o5])
        gate_four = jax.nn.sigmoid(proj[..., o5:])

        q = apply_axial_rope(rmsnorm(jnp.transpose(q, (0, 2, 3, 1, 4)), q_norm_g), cos, sin)
        k = apply_axial_rope(rmsnorm(jnp.transpose(k, (0, 2, 1, 3)), k_norm_g), cos, sin)
        v = jnp.transpose(v, (0, 2, 1, 3))
        attn = blocked_gqa(q, k, v) @ w_attn_out

        four = fourier_mix(f_in) @ w_fourier_out

        mixed = (gate_attn * attn + gate_four * four) @ w_mix_out
        x = x + gt1 * rmsnorm(mixed, g_post_mix)

        h2 = rmsnorm(x, g_pre_ffn) * (1.0 + sc2) + sh2
        ffn = hierarchical_moe(h2, w_group_router, b_group_router, w_expert_router, b_expert_router,
                               w_exp_gate, w_exp_up, w_exp_down)
        x = x + gt2 * rmsnorm(ffn, g_post_ffn)
    return x
```

```python
import functools
import math

import numpy as np
import jax
import jax.numpy as jnp
from jax import lax
from jax.experimental import pallas as pl
from jax.experimental.pallas import tpu as pltpu

F32 = jnp.float32
BF16 = jnp.bfloat16

D_MODEL = 2048
SEQ = 4096
N_HEADS = 16
N_KV_HEADS = 4
Q_PER_KV = N_HEADS // N_KV_HEADS
HEAD_DIM = 128
ROPE_THETA = 10000.0
GRID_W = 64
ATTN_Q_DIM = N_HEADS * HEAD_DIM
ATTN_KV_DIM = N_KV_HEADS * HEAD_DIM
N_FOURIER_GROUPS = 4
FOURIER_GROUP_DIM = 256
FOURIER_DIM = N_FOURIER_GROUPS * FOURIER_GROUP_DIM
IN_PROJ_DIM = ATTN_Q_DIM + 2 * ATTN_KV_DIM + FOURIER_DIM + 2 * D_MODEL
N_GROUPS = 4
EXPERTS_PER_GROUP = 8
N_EXPERTS = N_GROUPS * EXPERTS_PER_GROUP
EXPERT_FF = 512
NORM_EPS = 1e-6

LANES = 128
FFT_RADIX = 64
VMEM_LIMIT = 56 * 1024 * 1024

Q_PRESCALE = HEAD_DIM ** -0.5 * math.log2(math.e)

ADA_TN = 1024
INPROJ_TM = 512
INPROJ_TN = 512
ATTN_TQ = 512
ATTN_SUB = 128
POST_TM = 256
ROUTE_TB = 512
MOE_TB = 256
MOE_TM = 256


def _cparams(sem, vmem=VMEM_LIMIT):
    return pltpu.CompilerParams(dimension_semantics=sem, vmem_limit_bytes=vmem)


def _ada_kernel(c_ref, w_ref, b_ref, o_ref):
    cs = c_ref[...]
    s = cs * jax.nn.sigmoid(cs)
    o_ref[...] = jnp.dot(s.astype(BF16), w_ref[...].astype(BF16), preferred_element_type=F32) + b_ref[...]


def _ada(c_pad, w_ada, b_ada):
    n = w_ada.shape[1]
    return pl.pallas_call(
        _ada_kernel,
        grid=(n // ADA_TN,),
        in_specs=[
            pl.BlockSpec((8, D_MODEL), lambda j: (0, 0)),
            pl.BlockSpec((D_MODEL, ADA_TN), lambda j: (0, j)),
            pl.BlockSpec((1, ADA_TN), lambda j: (0, j)),
        ],
        out_specs=pl.BlockSpec((8, ADA_TN), lambda j: (0, j)),
        out_shape=jax.ShapeDtypeStruct((8, n), F32),
        compiler_params=_cparams(("arbitrary",)),
        name="ada",
    )(c_pad, w_ada, b_ada.reshape(1, n))


def _head_norm_rope(t, g, cos, sin_a, sin_b):
    ms = jnp.mean(t * t, axis=-1, keepdims=True)
    y = t * lax.rsqrt(ms + NORM_EPS) * g
    return y * cos + pltpu.roll(y, 96, 1) * sin_a + pltpu.roll(y, 32, 1) * sin_b


def _inproj_kernel(x_ref, mod_ref, g_ref, w_ref, cos_ref, sa_ref, sb_ref, qg_ref, kg_ref,
                   q_ref, k_ref, v_ref, f_ref, gate_ref, h_scr):
    j = pl.program_id(1)

    @pl.when(j == 0)
    def _():
        x = x_ref[...]
        ms = jnp.mean(x * x, axis=-1, keepdims=True)
        y = x * lax.rsqrt(ms + NORM_EPS) * g_ref[...]
        h_scr[...] = (y * (1.0 + mod_ref[1:2, :]) + mod_ref[0:1, :]).astype(BF16)

    r = jnp.dot(h_scr[...], w_ref[...], preferred_element_type=F32)
    heads_per_tile = INPROJ_TN // HEAD_DIM

    @pl.when(j < ATTN_Q_DIM // INPROJ_TN)
    def _():
        for hh in range(heads_per_tile):
            t = _head_norm_rope(r[:, hh * HEAD_DIM:(hh + 1) * HEAD_DIM], qg_ref[...],
                                cos_ref[...], sa_ref[...], sb_ref[...])
            q_ref[hh] = (t * Q_PRESCALE).astype(BF16)

    @pl.when(j == ATTN_Q_DIM // INPROJ_TN)
    def _():
        for hh in range(heads_per_tile):
            t = _head_norm_rope(r[:, hh * HEAD_DIM:(hh + 1) * HEAD_DIM], kg_ref[...],
                                cos_ref[...], sa_ref[...], sb_ref[...])
            k_ref[hh] = t.astype(BF16)

    @pl.when(j == ATTN_Q_DIM // INPROJ_TN + 1)
    def _():
        for hh in range(heads_per_tile):
            v_ref[hh] = r[:, hh * HEAD_DIM:(hh + 1) * HEAD_DIM].astype(BF16)

    f0 = (ATTN_Q_DIM + 2 * ATTN_KV_DIM) // INPROJ_TN
    g0 = f0 + FOURIER_DIM // INPROJ_TN

    @pl.when((j >= f0) & (j < g0))
    def _():
        for gg in range(INPROJ_TN // FOURIER_GROUP_DIM):
            f_ref[gg] = r[:, gg * FOURIER_GROUP_DIM:(gg + 1) * FOURIER_GROUP_DIM].astype(BF16)

    @pl.when(j >= g0)
    def _():
        gate_ref[...] = jax.nn.sigmoid(r).astype(BF16)


def _in_proj(x2, mod3, g_pre, w_in_bf, cos, sin_a, sin_b, qg, kg):
    t_tokens = x2.shape[0]
    tm, tn = INPROJ_TM, INPROJ_TN
    tiles_per_seq = SEQ // tm
    nq = ATTN_Q_DIM // tn
    f0 = (ATTN_Q_DIM + 2 * ATTN_KV_DIM) // tn
    nf = FOURIER_DIM // tn
    g0 = f0 + nf
    ng = 2 * D_MODEL // tn
    hpt = tn // HEAD_DIM
    gpt = tn // FOURIER_GROUP_DIM
    assert ATTN_KV_DIM == tn
    row_tbl = pl.BlockSpec((tm, HEAD_DIM), lambda i, j: (i % tiles_per_seq, 0))
    vec_hd = pl.BlockSpec((1, HEAD_DIM), lambda i, j: (0, 0))
    return pl.pallas_call(
        _inproj_kernel,
        grid=(t_tokens // tm, IN_PROJ_DIM // tn),
        in_specs=[
            pl.BlockSpec((tm, D_MODEL), lambda i, j: (i, 0)),
            pl.BlockSpec((None, 6, D_MODEL), lambda i, j: (i // tiles_per_seq, 0, 0)),
            pl.BlockSpec((1, D_MODEL), lambda i, j: (0, 0)),
            pl.BlockSpec((D_MODEL, tn), lambda i, j: (0, j)),
            row_tbl, row_tbl, row_tbl, vec_hd, vec_hd,
        ],
        out_specs=[
            pl.BlockSpec((hpt, tm, HEAD_DIM), lambda i, j: (jnp.minimum(j, nq - 1), i, 0)),
            pl.BlockSpec((hpt, tm, HEAD_DIM), lambda i, j: (0, i, 0)),
            pl.BlockSpec((hpt, tm, HEAD_DIM), lambda i, j: (0, i, 0)),
            pl.BlockSpec((gpt, tm, FOURIER_GROUP_DIM), lambda i, j: (jnp.clip(j - f0, 0, nf - 1), i, 0)),
            pl.BlockSpec((tm, tn), lambda i, j: (i, jnp.clip(j - g0, 0, ng - 1))),
        ],
        out_shape=[
            jax.ShapeDtypeStruct((N_HEADS, t_tokens, HEAD_DIM), BF16),
            jax.ShapeDtypeStruct((N_KV_HEADS, t_tokens, HEAD_DIM), BF16),
            jax.ShapeDtypeStruct((N_KV_HEADS, t_tokens, HEAD_DIM), BF16),
            jax.ShapeDtypeStruct((N_FOURIER_GROUPS, t_tokens, FOURIER_GROUP_DIM), BF16),
            jax.ShapeDtypeStruct((t_tokens, 2 * D_MODEL), BF16),
        ],
        scratch_shapes=[pltpu.VMEM((tm, D_MODEL), BF16)],
        compiler_params=_cparams(("arbitrary", "arbitrary")),
        name="in_proj",
    )(x2, mod3, g_pre.reshape(1, D_MODEL), w_in_bf, cos, sin_a, sin_b,
      qg.reshape(1, HEAD_DIM), kg.reshape(1, HEAD_DIM))


def _attn_kernel(q_ref, k_ref, v_ref, o_ref):
    k = k_ref[...]
    v = v_ref[...]

    def body(r, carry):
        r0 = pl.multiple_of(r * ATTN_SUB, ATTN_SUB)
        q = q_ref[:, pl.ds(r0, ATTN_SUB), :].reshape(Q_PER_KV * ATTN_SUB, HEAD_DIM)
        s = lax.dot_general(q, k, (((1,), (1,)), ((), ())), preferred_element_type=F32)
        m = jnp.max(s, axis=-1, keepdims=True)
        p = jnp.exp2(s - m)
        l = jnp.sum(p, axis=-1, keepdims=True)
        o = jnp.dot(p.astype(BF16), v, preferred_element_type=F32) / l
        for hh in range(Q_PER_KV):
            o_ref[pl.ds(r0, ATTN_SUB), hh * HEAD_DIM:(hh + 1) * HEAD_DIM] = (
                o[hh * ATTN_SUB:(hh + 1) * ATTN_SUB].astype(BF16))
        return carry

    lax.fori_loop(0, ATTN_TQ // ATTN_SUB, body, 0)


def _attention(q, k, v):
    t_tokens = q.shape[1]
    n_b = t_tokens // SEQ
    qt = SEQ // ATTN_TQ
    return pl.pallas_call(
        _attn_kernel,
        grid=(n_b, N_KV_HEADS, qt),
        in_specs=[
            pl.BlockSpec((Q_PER_KV, ATTN_TQ, HEAD_DIM), lambda b, h, i: (h, b * qt + i, 0)),
            pl.BlockSpec((None, SEQ, HEAD_DIM), lambda b, h, i: (h, b, 0)),
            pl.BlockSpec((None, SEQ, HEAD_DIM), lambda b, h, i: (h, b, 0)),
        ],
        out_specs=pl.BlockSpec((ATTN_TQ, Q_PER_KV * HEAD_DIM), lambda b, h, i: (b * qt + i, h)),
        out_shape=jax.ShapeDtypeStruct((t_tokens, ATTN_Q_DIM), BF16),
        compiler_params=_cparams(("arbitrary", "arbitrary", "arbitrary")),
        name="attention",
    )(q, k, v)


def _fourier_tables():
    r = FFT_RADIX
    n = FOURIER_GROUP_DIM
    ch = np.arange(n)
    ang_c = 2.0 * np.pi * np.outer(ch, ch) / n
    scale = 1.0 / math.sqrt(SEQ * n)
    w_ch = np.concatenate([np.cos(ang_c), -np.sin(ang_c)], axis=1) * scale
    a = np.arange(r)
    ang1 = 2.0 * np.pi * np.outer(a, a) / r
    c1, s1 = np.cos(ang1), np.sin(ang1)
    w1 = np.zeros((r, 2, 2, r))
    w1[:, 0, 0, :] = c1
    w1[:, 0, 1, :] = s1
    w1[:, 1, 0, :] = -s1
    w1[:, 1, 1, :] = c1
    w1 = w1.reshape(2 * r, 2 * r)
    sp = (np.arange(r)[:, None] + r * np.arange(r)[None, :])
    th = 2.0 * np.pi * sp[:, :, None] * np.arange(r)[None, None, :] / SEQ
    w2 = np.concatenate([np.cos(th), np.sin(th)], axis=2)
    return (jnp.asarray(w_ch, dtype=BF16), jnp.asarray(w1, dtype=BF16), jnp.asarray(w2, dtype=BF16))


def _fourier_kernel(f_ref, wch_ref, w1_ref, w2_ref, o_ref, x_scr, y_scr, o_scr):
    r = FFT_RADIX
    n = FOURIER_GROUP_DIM
    z = jnp.dot(f_ref[...], wch_ref[...], preferred_element_type=F32)
    x_scr[0:r] = z[:, :n].astype(BF16).reshape(r, r, n)
    x_scr[r:2 * r] = z[:, n:].astype(BF16).reshape(r, r, n)
    y = lax.dot_general(w1_ref[...], x_scr[...], (((1,), (0,)), ((), ())),
                        preferred_element_type=F32)
    y_scr[...] = y.astype(BF16)

    def body(c, carry):
        slab = y_scr[pl.ds(2 * c, 2)].reshape(2 * r, n)
        res = jnp.dot(w2_ref[c], slab, preferred_element_type=F32)
        for s in range(n // LANES):
            o_scr[s, pl.ds(c, r, stride=r), :] = res[:, s * LANES:(s + 1) * LANES]
        return carry

    lax.fori_loop(0, r, body, 0)
    for s in range(n // LANES):
        o_ref[:, s * LANES:(s + 1) * LANES] = o_scr[s].astype(BF16)


def _fourier(f):
    t_tokens = f.shape[1]
    n_b = t_tokens // SEQ
    r = FFT_RADIX
    n = FOURIER_GROUP_DIM
    w_ch, w1, w2 = _fourier_tables()
    blk = pl.BlockSpec((None, SEQ, n), lambda g, b: (g, b, 0))
    return pl.pallas_call(
        _fourier_kernel,
        grid=(N_FOURIER_GROUPS, n_b),
        in_specs=[
            blk,
            pl.BlockSpec((n, 2 * n), lambda g, b: (0, 0)),
            pl.BlockSpec((2 * r, 2 * r), lambda g, b: (0, 0)),
            pl.BlockSpec((r, r, 2 * r), lambda g, b: (0, 0, 0)),
        ],
        out_specs=blk,
        out_shape=jax.ShapeDtypeStruct((N_FOURIER_GROUPS, t_tokens, n), BF16),
        scratch_shapes=[
            pltpu.VMEM((2 * r, r, n), BF16),
            pltpu.VMEM((2 * r, r, n), BF16),
            pltpu.VMEM((n // LANES, SEQ, LANES), F32),
        ],
        compiler_params=_cparams(("arbitrary", "arbitrary")),
        name="fourier",
    )(f, w_ch, w1, w2)


def _post_mix_kernel(attn_ref, four_ref, gate_ref, x_ref, mod_ref, gpost_ref, gpre_ref,
                     wao_ref, wfo_ref, wmo_ref, wr_ref, br_ref,
                     x1_ref, h2_ref, logit_ref):
    a = jnp.dot(attn_ref[...], wao_ref[...], preferred_element_type=F32)
    fo = jnp.dot(four_ref[0], wfo_ref[0], preferred_element_type=F32)
    for g in range(1, N_FOURIER_GROUPS):
        fo = fo + jnp.dot(four_ref[g], wfo_ref[g], preferred_element_type=F32)
    ga = gate_ref[:, :D_MODEL].astype(F32)
    gf = gate_ref[:, D_MODEL:].astype(F32)
    m = (ga * a + gf * fo).astype(BF16)
    mixed = jnp.dot(m, wmo_ref[...], preferred_element_type=F32)
    ms = jnp.mean(mixed * mixed, axis=-1, keepdims=True)
    y = mixed * lax.rsqrt(ms + NORM_EPS) * gpost_ref[...]
    x1 = x_ref[...] + mod_ref[2:3, :] * y
    x1_ref[...] = x1
    ms1 = jnp.mean(x1 * x1, axis=-1, keepdims=True)
    h2 = (x1 * lax.rsqrt(ms1 + NORM_EPS) * gpre_ref[...]) * (1.0 + mod_ref[4:5, :]) + mod_ref[3:4, :]
    h2_ref[...] = h2
    logit_ref[...] = jnp.dot(h2.astype(BF16), wr_ref[...], preferred_element_type=F32) + br_ref[...]


def _post_mix(attn, four, gates, x2, mod3, g_post, g_pre, wao, wfo4, wmo, w_router, b_router):
    t_tokens = x2.shape[0]
    tm = POST_TM
    tiles_per_seq = SEQ // tm
    n = FOURIER_GROUP_DIM
    const2 = lambda i: (0, 0)
    resident = dict(pipeline_mode=pl.Buffered(1))
    return pl.pallas_call(
        _post_mix_kernel,
        grid=(t_tokens // tm,),
        in_specs=[
            pl.BlockSpec((tm, ATTN_Q_DIM), lambda i: (i, 0)),
            pl.BlockSpec((N_FOURIER_GROUPS, tm, n), lambda i: (0, i, 0)),
            pl.BlockSpec((tm, 2 * D_MODEL), lambda i: (i, 0)),
            pl.BlockSpec((tm, D_MODEL), lambda i: (i, 0)),
            pl.BlockSpec((None, 6, D_MODEL), lambda i: (i // tiles_per_seq, 0, 0)),
            pl.BlockSpec((1, D_MODEL), const2),
            pl.BlockSpec((1, D_MODEL), const2),
            pl.BlockSpec((ATTN_Q_DIM, D_MODEL), const2, **resident),
            pl.BlockSpec((N_FOURIER_GROUPS, n, D_MODEL), lambda i: (0, 0, 0), **resident),
            pl.BlockSpec((D_MODEL, D_MODEL), const2, **resident),
            pl.BlockSpec((D_MODEL, LANES), const2, **resident),
            pl.BlockSpec((1, LANES), const2),
        ],
        out_specs=[
            pl.BlockSpec((tm, D_MODEL), lambda i: (i, 0)),
            pl.BlockSpec((tm, D_MODEL), lambda i: (i, 0)),
            pl.BlockSpec((tm, LANES), lambda i: (i, 0)),
        ],
        out_shape=[
            jax.ShapeDtypeStruct((t_tokens, D_MODEL), F32),
            jax.ShapeDtypeStruct((t_tokens, D_MODEL), F32),
            jax.ShapeDtypeStruct((t_tokens, LANES), F32),
        ],
        compiler_params=_cparams(("arbitrary",)),
        name="post_mix",
    )(attn, four, gates, x2, mod3, g_post.reshape(1, D_MODEL), g_pre.reshape(1, D_MODEL),
      wao, wfo4, wmo, w_router, b_router)


def _first_argmax(vals, lane):
    mx = jnp.max(vals, axis=-1, keepdims=True)
    idx = jnp.min(jnp.where(vals == mx, lane, float(LANES)), axis=-1, keepdims=True)
    return mx, idx


def _route_kernel(logit_ref, tri_ref, out_ref, cnt_ref, carry_scr):
    i = pl.program_id(0)

    @pl.when(i == 0)
    def _():
        carry_scr[...] = jnp.zeros_like(carry_scr)

    lg = logit_ref[...]
    lane = lax.broadcasted_iota(jnp.int32, lg.shape, 1).astype(F32)
    neg = jnp.float32(-jnp.inf)
    gl = jnp.where(lane < N_GROUPS, lg, neg)
    gmax, gidx = _first_argmax(gl, lane)
    g_w = 1.0 / jnp.sum(jnp.exp(gl - gmax), axis=-1, keepdims=True)
    lo = N_GROUPS + EXPERTS_PER_GROUP * gidx
    el = jnp.where((lane >= lo) & (lane < lo + EXPERTS_PER_GROUP), lg, neg)
    m1, i1 = _first_argmax(el, lane)
    el2 = jnp.where(lane == i1, neg, el)
    m2, i2 = _first_argmax(el2, lane)
    p2 = jnp.exp(m2 - m1)
    w1 = g_w / (1.0 + p2)
    w2 = g_w * p2 / (1.0 + p2)
    oh = jnp.where((lane == i1) | (lane == i2), 1.0, 0.0)
    before = jnp.dot(tri_ref[...], oh.astype(BF16), preferred_element_type=F32) + carry_scr[...]
    r1 = jnp.sum(jnp.where(lane == i1, before, 0.0), axis=-1, keepdims=True)
    r2 = jnp.sum(jnp.where(lane == i2, before, 0.0), axis=-1, keepdims=True)
    carry_scr[...] = carry_scr[...] + jnp.sum(oh, axis=0, keepdims=True)
    e1 = i1 - N_GROUPS
    e2 = i2 - N_GROUPS
    out = jnp.where(lane == 0, e1, 0.0)
    out = jnp.where(lane == 1, e2, out)
    out = jnp.where(lane == 2, r1, out)
    out = jnp.where(lane == 3, r2, out)
    out = jnp.where(lane == 4, w1, out)
    out = jnp.where(lane == 5, w2, out)
    out_ref[...] = out
    cnt_ref[...] = jnp.broadcast_to(carry_scr[...], cnt_ref.shape)


def _route(logits):
    t_tokens = logits.shape[0]
    tb = ROUTE_TB
    tri = jnp.asarray(np.tril(np.ones((tb, tb), np.float32), -1), dtype=BF16)
    return pl.pallas_call(
        _route_kernel,
        grid=(t_tokens // tb,),
        in_specs=[
            pl.BlockSpec((tb, LANES), lambda i: (i, 0)),
            pl.BlockSpec((tb, tb), lambda i: (0, 0)),
        ],
        out_specs=[
            pl.BlockSpec((tb, LANES), lambda i: (i, 0)),
            pl.BlockSpec((8, LANES), lambda i: (0, 0)),
        ],
        out_shape=[
            jax.ShapeDtypeStruct((t_tokens, LANES), F32),
            jax.ShapeDtypeStruct((8, LANES), F32),
        ],
        scratch_shapes=[pltpu.VMEM((1, LANES), F32)],
        compiler_params=_cparams(("arbitrary",)),
        name="route",
    )(logits, tri)


def _row_copy(src_hbm, src_row, dst_hbm, dst_row, sem):
    return pltpu.make_async_copy(src_hbm.at[pl.ds(src_row, 1), :], dst_hbm.at[pl.ds(dst_row, 1), :], sem)


def _dispatch_kernel(p1_ref, p2_ref, h_hbm, xs_hbm, sem):
    base = pl.program_id(0) * MOE_TB

    def start(j, carry):
        _row_copy(h_hbm, base + j, xs_hbm, p1_ref[0, 0, j], sem).start()
        _row_copy(h_hbm, base + j, xs_hbm, p2_ref[0, 0, j], sem).start()
        return carry

    lax.fori_loop(0, MOE_TB, start, 0)

    def wait(j, carry):
        _row_copy(h_hbm, base + j, xs_hbm, p1_ref[0, 0, j], sem).wait()
        _row_copy(h_hbm, base + j, xs_hbm, p2_ref[0, 0, j], sem).wait()
        return carry

    lax.fori_loop(0, MOE_TB, wait, 0)


def _dispatch(pos1, pos2, h2):
    t_tokens = h2.shape[0]
    nb = t_tokens // MOE_TB
    smem_blk = pl.BlockSpec((1, 1, MOE_TB), lambda i: (i, 0, 0), memory_space=pltpu.SMEM)
    return pl.pallas_call(
        _dispatch_kernel,
        grid=(nb,),
        in_specs=[smem_blk, smem_blk, pl.BlockSpec(memory_space=pl.ANY)],
        out_specs=pl.BlockSpec(memory_space=pl.ANY),
        out_shape=jax.ShapeDtypeStruct((2 * t_tokens, D_MODEL), F32),
        scratch_shapes=[pltpu.SemaphoreType.DMA(())],
        compiler_params=_cparams(("arbitrary",)),
        name="dispatch",
    )(pos1.reshape(nb, 1, MOE_TB), pos2.reshape(nb, 1, MOE_TB), h2)


def _expert_kernel(tile_ref, exp_ref, start_ref, end_ref, nitem_ref,
                   xs_ref, wg_ref, wu_ref, wd_ref, ys_ref):
    k = pl.program_id(0)
    tile = tile_ref[k]
    e = exp_ref[k]
    prev_tile = tile_ref[jnp.maximum(k - 1, 0)]
    first_visit = (k == 0) | (tile != prev_tile)

    @pl.when(first_visit)
    def _():
        ys_ref[...] = jnp.zeros_like(ys_ref)

    @pl.when(k < nitem_ref[0])
    def _():
        x = xs_ref[...].astype(BF16)
        g = jnp.dot(x, wg_ref[...].astype(BF16), preferred_element_type=F32)
        u = jnp.dot(x, wu_ref[...].astype(BF16), preferred_element_type=F32)
        hmid = (g * jax.nn.sigmoid(g) * u).astype(BF16)
        y = jnp.dot(hmid, wd_ref[...].astype(BF16), preferred_element_type=F32)
        row = tile * MOE_TM + lax.broadcasted_iota(jnp.int32, (MOE_TM, 1), 0)
        valid = (row >= start_ref[e]) & (row < end_ref[e])
        ys_ref[...] += jnp.where(valid, y, 0.0)


def _experts(item_tile, item_exp, starts, ends, n_items, xs, wg, wu, wd):
    n_rows = xs.shape[0]
    n_tiles = n_rows // MOE_TM
    max_items = n_tiles + N_EXPERTS - 1
    grid_spec = pltpu.PrefetchScalarGridSpec(
        num_scalar_prefetch=5,
        grid=(max_items,),
        in_specs=[
            pl.BlockSpec((MOE_TM, D_MODEL), lambda k, t, e, s, en, n: (t[k], 0)),
            pl.BlockSpec((None, D_MODEL, EXPERT_FF), lambda k, t, e, s, en, n: (e[k], 0, 0)),
            pl.BlockSpec((None, D_MODEL, EXPERT_FF), lambda k, t, e, s, en, n: (e[k], 0, 0)),
            pl.BlockSpec((None, EXPERT_FF, D_MODEL), lambda k, t, e, s, en, n: (e[k], 0, 0)),
        ],
        out_specs=pl.BlockSpec((MOE_TM, D_MODEL), lambda k, t, e, s, en, n: (t[k], 0)),
    )
    return pl.pallas_call(
        _expert_kernel,
        grid_spec=grid_spec,
        out_shape=jax.ShapeDtypeStruct((n_rows, D_MODEL), F32),
        compiler_params=_cparams(("arbitrary",)),
        name="experts",
    )(item_tile, item_exp, starts, ends, n_items, xs, wg, wu, wd)


def _expert_items(counts):
    ends = jnp.cumsum(counts)
    starts = ends - counts
    first_tile = starts // MOE_TM
    last_tile = (ends - 1) // MOE_TM
    n_it = jnp.where(counts > 0, last_tile - first_tile + 1, 0)
    it_end = jnp.cumsum(n_it)
    it_start = it_end - n_it
    total = it_end[-1]
    return starts, ends, first_tile, it_start, it_end, total


def _combine_kernel(p1_ref, p2_ref, ys_hbm, route_ref, x1_ref, mod_ref, g_ref, o_ref, y1_scr, y2_scr, sem):
    def start(j, carry):
        pltpu.make_async_copy(ys_hbm.at[pl.ds(p1_ref[0, 0, j], 1), :], y1_scr.at[pl.ds(j, 1), :], sem).start()
        pltpu.make_async_copy(ys_hbm.at[pl.ds(p2_ref[0, 0, j], 1), :], y2_scr.at[pl.ds(j, 1), :], sem).start()
        return carry

    lax.fori_loop(0, MOE_TB, start, 0)

    def wait(j, carry):
        pltpu.make_async_copy(ys_hbm.at[pl.ds(p1_ref[0, 0, j], 1), :], y1_scr.at[pl.ds(j, 1), :], sem).wait()
        pltpu.make_async_copy(ys_hbm.at[pl.ds(p2_ref[0, 0, j], 1), :], y2_scr.at[pl.ds(j, 1), :], sem).wait()
        return carry

    lax.fori_loop(0, MOE_TB, wait, 0)
    w1 = route_ref[:, 4:5]
    w2 = route_ref[:, 5:6]
    ffn = w1 * y1_scr[...] + w2 * y2_scr[...]
    ms = jnp.mean(ffn * ffn, axis=-1, keepdims=True)
    y = ffn * lax.rsqrt(ms + NORM_EPS) * g_ref[...]
    o_ref[...] = x1_ref[...] + mod_ref[5:6, :] * y


def _combine(pos1, pos2, ys, route, x1, mod3, g_post):
    t_tokens = x1.shape[0]
    tb = MOE_TB
    nb = t_tokens // tb
    tiles_per_seq = SEQ // tb
    smem_blk = pl.BlockSpec((1, 1, tb), lambda i: (i, 0, 0), memory_space=pltpu.SMEM)
    return pl.pallas_call(
        _combine_kernel,
        grid=(nb,),
        in_specs=[
            smem_blk, smem_blk,
            pl.BlockSpec(memory_space=pl.ANY),
            pl.BlockSpec((tb, LANES), lambda i: (i, 0)),
            pl.BlockSpec((tb, D_MODEL), lambda i: (i, 0)),
            pl.BlockSpec((None, 6, D_MODEL), lambda i: (i // tiles_per_seq, 0, 0)),
            pl.BlockSpec((1, D_MODEL), lambda i: (0, 0)),
        ],
        out_specs=pl.BlockSpec((tb, D_MODEL), lambda i: (i, 0)),
        out_shape=jax.ShapeDtypeStruct((t_tokens, D_MODEL), F32),
        scratch_shapes=[
            pltpu.VMEM((tb, D_MODEL), F32),
            pltpu.VMEM((tb, D_MODEL), F32),
            pltpu.SemaphoreType.DMA(()),
        ],
        compiler_params=_cparams(("arbitrary",)),
        name="combine",
    )(pos1.reshape(nb, 1, tb), pos2.reshape(nb, 1, tb), ys, route, x1, mod3, g_post.reshape(1, D_MODEL))


def _rope_tables():
    rows = SEQ // GRID_W
    row = jnp.repeat(jnp.arange(rows, dtype=F32), GRID_W)
    col = jnp.tile(jnp.arange(GRID_W, dtype=F32), rows)
    n_freq = HEAD_DIM // 4
    inv = ROPE_THETA ** (-jnp.arange(n_freq, dtype=F32) / n_freq)
    ang_r = row[:, None] * inv
    ang_c = col[:, None] * inv
    ang = jnp.concatenate([ang_r, ang_r, ang_c, ang_c], axis=-1)
    cos, sin = jnp.cos(ang), jnp.sin(ang)
    first = (jnp.arange(HEAD_DIM) % (2 * n_freq)) < n_freq
    sin_a = jnp.where(first, -sin, 0.0)
    sin_b = jnp.where(first, 0.0, sin)
    return cos, sin_a, sin_b


def kernel(x, c, w_ada, b_ada, g_pre_mix, g_post_mix, w_in, q_norm_g, k_norm_g, w_attn_out, w_fourier_out,
           w_mix_out, g_pre_ffn, g_post_ffn, w_group_router, b_group_router, w_expert_router, b_expert_router,
           w_exp_gate, w_exp_up, w_exp_down):
    n_b, seq, d = x.shape
    assert seq == SEQ and d == D_MODEL
    t_tokens = n_b * seq
    x2 = x.reshape(t_tokens, d)

    c_pad = jnp.zeros((8, d), F32).at[:n_b].set(c)
    mod = _ada(c_pad, w_ada, b_ada)[:n_b]
    mod3 = mod.reshape(n_b, 6, d)

    cos, sin_a, sin_b = _rope_tables()
    q, k, v, f, gates = _in_proj(x2, mod3, g_pre_mix, w_in.astype(BF16), cos, sin_a, sin_b, q_norm_g, k_norm_g)
    attn = _attention(q, k, v)
    four = _fourier(f)

    n_r = N_GROUPS + N_EXPERTS
    w_router = jnp.zeros((d, LANES), F32).at[:, :N_GROUPS].set(w_group_router).at[:, N_GROUPS:n_r].set(w_expert_router)
    b_router = jnp.zeros((1, LANES), F32).at[0, :N_GROUPS].set(b_group_router).at[0, N_GROUPS:n_r].set(b_expert_router)
    x1, h2, logits = _post_mix(
        attn, four, gates, x2, mod3, g_post_mix, g_pre_ffn,
        w_attn_out.astype(BF16),
        w_fourier_out.astype(BF16).reshape(N_FOURIER_GROUPS, FOURIER_GROUP_DIM, d),
        w_mix_out.astype(BF16), w_router.astype(BF16), b_router)

    route, cnt = _route(logits)
    counts = cnt[0, N_GROUPS:n_r].astype(jnp.int32)
    e1 = route[:, 0].astype(jnp.int32)
    e2 = route[:, 1].astype(jnp.int32)
    starts, ends, first_tile, it_start, it_end, total = _expert_items(counts)
    pos1 = starts[e1] + route[:, 2].astype(jnp.int32)
    pos2 = starts[e2] + route[:, 3].astype(jnp.int32)

    xs = _dispatch(pos1, pos2, h2)

    n_tiles = 2 * t_tokens // MOE_TM
    max_items = n_tiles + N_EXPERTS - 1
    kk = jnp.minimum(jnp.arange(max_items, dtype=jnp.int32), total - 1)
    item_exp = jnp.sum((it_end[None, :] <= kk[:, None]).astype(jnp.int32), axis=1)
    item_tile = first_tile[item_exp] + kk - it_start[item_exp]
    ys = _experts(item_tile.astype(jnp.int32), item_exp.astype(jnp.int32), starts.astype(jnp.int32),
                  ends.astype(jnp.int32), total.reshape(1).astype(jnp.int32), xs, w_exp_gate, w_exp_up, w_exp_down)

    out = _combine(pos1, pos2, ys, route, x1, mod3, g_post_ffn)
    return out.reshape(n_b, seq, d)
```

```python
import functools
import math

import numpy as np
import jax
import jax.numpy as jnp
from jax import lax
from jax.experimental import pallas as pl
from jax.experimental.pallas import tpu as pltpu

F32 = jnp.float32
BF16 = jnp.bfloat16

D_MODEL = 2048
SEQ = 4096
N_HEADS = 16
N_KV_HEADS = 4
Q_PER_KV = N_HEADS // N_KV_HEADS
HEAD_DIM = 128
ROPE_THETA = 10000.0
GRID_W = 64
ATTN_Q_DIM = N_HEADS * HEAD_DIM
ATTN_KV_DIM = N_KV_HEADS * HEAD_DIM
N_FOURIER_GROUPS = 4
FOURIER_GROUP_DIM = 256
FOURIER_DIM = N_FOURIER_GROUPS * FOURIER_GROUP_DIM
IN_PROJ_DIM = ATTN_Q_DIM + 2 * ATTN_KV_DIM + FOURIER_DIM + 2 * D_MODEL
N_GROUPS = 4
EXPERTS_PER_GROUP = 8
N_EXPERTS = N_GROUPS * EXPERTS_PER_GROUP
EXPERT_FF = 512
NORM_EPS = 1e-6

LANES = 128
FFT_RADIX = 64
VMEM_LIMIT = 56 * 1024 * 1024

Q_PRESCALE = HEAD_DIM ** -0.5 * math.log2(math.e)

ADA_TN = 1024
INPROJ_TM = 512
INPROJ_TN = 512
VT_ROWS = HEAD_DIM + 16
ATTN_TQ = 256
ATTN_SUB = 128
ATTN_TK = 512
POST_TM = 256
ROUTE_TB = 512
MOE_TB = 256
MOE_TM = 256


def _cparams(sem, vmem=VMEM_LIMIT, flags=None):
    return pltpu.CompilerParams(dimension_semantics=sem, vmem_limit_bytes=vmem, flags=flags)


def _ada_kernel(c_ref, w_ref, b_ref, o_ref):
    cs = c_ref[...]
    s = cs * jax.nn.sigmoid(cs)
    o_ref[...] = jnp.dot(s.astype(BF16), w_ref[...].astype(BF16), preferred_element_type=F32) + b_ref[...]


def _ada(c_pad, w_ada, b_ada):
    n = w_ada.shape[1]
    return pl.pallas_call(
        _ada_kernel,
        grid=(n // ADA_TN,),
        in_specs=[
            pl.BlockSpec((8, D_MODEL), lambda j: (0, 0)),
            pl.BlockSpec((D_MODEL, ADA_TN), lambda j: (0, j)),
            pl.BlockSpec((1, ADA_TN), lambda j: (0, j)),
        ],
        out_specs=pl.BlockSpec((8, ADA_TN), lambda j: (0, j)),
        out_shape=jax.ShapeDtypeStruct((8, n), F32),
        compiler_params=_cparams(("arbitrary",)),
        name="ada",
    )(c_pad, w_ada, b_ada.reshape(1, n))


def _head_norm_rope(t, g, cos, sin_a, sin_b):
    ms = jnp.mean(t * t, axis=-1, keepdims=True)
    y = t * lax.rsqrt(ms + NORM_EPS) * g
    return y * cos + pltpu.roll(y, 96, 1) * sin_a + pltpu.roll(y, 32, 1) * sin_b


def _inproj_kernel(x_ref, mod_ref, g_ref, w_ref, cos_ref, sa_ref, sb_ref, qg_ref, kg_ref,
                   q_ref, k_ref, v_ref, f_ref, gate_ref, h_scr):
    j = pl.program_id(1)

    @pl.when(j == 0)
    def _():
        x = x_ref[...]
        ms = jnp.mean(x * x, axis=-1, keepdims=True)
        y = x * lax.rsqrt(ms + NORM_EPS) * g_ref[...]
        h_scr[...] = (y * (1.0 + mod_ref[1:2, :]) + mod_ref[0:1, :]).astype(BF16)

    r = jnp.dot(h_scr[...], w_ref[...], preferred_element_type=F32)
    heads_per_tile = INPROJ_TN // HEAD_DIM

    @pl.when(j < ATTN_Q_DIM // INPROJ_TN)
    def _():
        for hh in range(heads_per_tile):
            t = _head_norm_rope(r[:, hh * HEAD_DIM:(hh + 1) * HEAD_DIM], qg_ref[...],
                                cos_ref[...], sa_ref[...], sb_ref[...])
            q_ref[hh] = (t * Q_PRESCALE).astype(BF16)

    @pl.when(j == ATTN_Q_DIM // INPROJ_TN)
    def _():
        for hh in range(heads_per_tile):
            t = _head_norm_rope(r[:, hh * HEAD_DIM:(hh + 1) * HEAD_DIM], kg_ref[...],
                                cos_ref[...], sa_ref[...], sb_ref[...])
            k_ref[hh] = t.astype(BF16)

    @pl.when(j == ATTN_Q_DIM // INPROJ_TN + 1)
    def _():
        for hh in range(heads_per_tile):
            v_ref[hh, 0:HEAD_DIM, :] = r[:, hh * HEAD_DIM:(hh + 1) * HEAD_DIM].T.astype(BF16)
            pad_rows = VT_ROWS - HEAD_DIM
            ones_row = lax.broadcasted_iota(jnp.int32, (pad_rows, INPROJ_TM), 0) == 0
            v_ref[hh, HEAD_DIM:VT_ROWS, :] = jnp.where(ones_row, 1.0, 0.0).astype(BF16)

    f0 = (ATTN_Q_DIM + 2 * ATTN_KV_DIM) // INPROJ_TN
    g0 = f0 + FOURIER_DIM // INPROJ_TN

    @pl.when((j >= f0) & (j < g0))
    def _():
        for gg in range(INPROJ_TN // FOURIER_GROUP_DIM):
            f_ref[gg] = r[:, gg * FOURIER_GROUP_DIM:(gg + 1) * FOURIER_GROUP_DIM].astype(BF16)

    @pl.when(j >= g0)
    def _():
        gate_ref[...] = jax.nn.sigmoid(r).astype(BF16)


def _in_proj(x2, mod3, g_pre, w_in_bf, cos, sin_a, sin_b, qg, kg):
    t_tokens = x2.shape[0]
    tm, tn = INPROJ_TM, INPROJ_TN
    tiles_per_seq = SEQ // tm
    nq = ATTN_Q_DIM // tn
    f0 = (ATTN_Q_DIM + 2 * ATTN_KV_DIM) // tn
    nf = FOURIER_DIM // tn
    g0 = f0 + nf
    ng = 2 * D_MODEL // tn
    hpt = tn // HEAD_DIM
    gpt = tn // FOURIER_GROUP_DIM
    assert ATTN_KV_DIM == tn
    row_tbl = pl.BlockSpec((tm, HEAD_DIM), lambda i, j: (i % tiles_per_seq, 0))
    vec_hd = pl.BlockSpec((1, HEAD_DIM), lambda i, j: (0, 0))
    return pl.pallas_call(
        _inproj_kernel,
        grid=(t_tokens // tm, IN_PROJ_DIM // tn),
        in_specs=[
            pl.BlockSpec((tm, D_MODEL), lambda i, j: (i, 0)),
            pl.BlockSpec((None, 6, D_MODEL), lambda i, j: (i // tiles_per_seq, 0, 0)),
            pl.BlockSpec((1, D_MODEL), lambda i, j: (0, 0)),
            pl.BlockSpec((D_MODEL, tn), lambda i, j: (0, j)),
            row_tbl, row_tbl, row_tbl, vec_hd, vec_hd,
        ],
        out_specs=[
            pl.BlockSpec((hpt, tm, HEAD_DIM), lambda i, j: (jnp.minimum(j, nq - 1), i, 0)),
            pl.BlockSpec((hpt, tm, HEAD_DIM), lambda i, j: (0, i, 0)),
            pl.BlockSpec((hpt, VT_ROWS, tm), lambda i, j: (0, 0, i)),
            pl.BlockSpec((gpt, tm, FOURIER_GROUP_DIM), lambda i, j: (jnp.clip(j - f0, 0, nf - 1), i, 0)),
            pl.BlockSpec((tm, tn), lambda i, j: (i, jnp.clip(j - g0, 0, ng - 1))),
        ],
        out_shape=[
            jax.ShapeDtypeStruct((N_HEADS, t_tokens, HEAD_DIM), BF16),
            jax.ShapeDtypeStruct((N_KV_HEADS, t_tokens, HEAD_DIM), BF16),
            jax.ShapeDtypeStruct((N_KV_HEADS, VT_ROWS, t_tokens), BF16),
            jax.ShapeDtypeStruct((N_FOURIER_GROUPS, t_tokens, FOURIER_GROUP_DIM), BF16),
            jax.ShapeDtypeStruct((t_tokens, 2 * D_MODEL), BF16),
        ],
        scratch_shapes=[pltpu.VMEM((tm, D_MODEL), BF16)],
        compiler_params=_cparams(("arbitrary", "arbitrary")),
        name="in_proj",
    )(x2, mod3, g_pre.reshape(1, D_MODEL), w_in_bf, cos, sin_a, sin_b,
      qg.reshape(1, HEAD_DIM), kg.reshape(1, HEAD_DIM))


def _attn_kernel(q_ref, k_ref, vt_ref, o_ref, s_scr):
    n_chunks = SEQ // ATTN_TK
    n_sub = ATTN_TQ // ATTN_SUB

    def load_q(sub):
        return q_ref[:, sub * ATTN_SUB:(sub + 1) * ATTN_SUB, :].reshape(Q_PER_KV * ATTN_SUB, HEAD_DIM)

    def scores(q, c, slot):
        k_c = k_ref[c * ATTN_TK:(c + 1) * ATTN_TK, :]
        s_scr[slot] = lax.dot_general(k_c, q, (((1,), (1,)), ((), ())), preferred_element_type=F32)

    q = load_q(0)
    scores(q, 0, 0)
    unit = 0
    for sub in range(n_sub):
        q_next = load_q(sub + 1) if sub + 1 < n_sub else None
        m = acc = None
        for c in range(n_chunks):
            if c + 1 < n_chunks:
                scores(q, c + 1, (unit + 1) % 2)
            elif q_next is not None:
                scores(q_next, 0, (unit + 1) % 2)
            s = s_scr[unit % 2]
            vt_c = vt_ref[:, c * ATTN_TK:(c + 1) * ATTN_TK]
            m_c = jnp.max(s, axis=0, keepdims=True)
            if c == 0:
                m = m_c
                acc = jnp.dot(vt_c, jnp.exp2(s - m).astype(BF16), preferred_element_type=F32)
            else:
                m_new = jnp.maximum(m, m_c)
                alpha = jnp.exp2(m - m_new)
                acc = alpha * acc + jnp.dot(vt_c, jnp.exp2(s - m_new).astype(BF16), preferred_element_type=F32)
                m = m_new
            unit += 1
        o_t = acc[:HEAD_DIM] * (1.0 / acc[HEAD_DIM:HEAD_DIM + 1])
        r0 = sub * ATTN_SUB
        for hh in range(Q_PER_KV):
            o_ref[r0:r0 + ATTN_SUB, hh * HEAD_DIM:(hh + 1) * HEAD_DIM] = (
                o_t[:, hh * ATTN_SUB:(hh + 1) * ATTN_SUB].T.astype(BF16))
        q = q_next


def _attention(q, k, vt):
    t_tokens = q.shape[1]
    n_b = t_tokens // SEQ
    qt = SEQ // ATTN_TQ
    return pl.pallas_call(
        _attn_kernel,
        grid=(n_b, N_KV_HEADS, qt),
        in_specs=[
            pl.BlockSpec((Q_PER_KV, ATTN_TQ, HEAD_DIM), lambda b, h, i: (h, b * qt + i, 0)),
            pl.BlockSpec((None, SEQ, HEAD_DIM), lambda b, h, i: (h, b, 0)),
            pl.BlockSpec((None, VT_ROWS, SEQ), lambda b, h, i: (h, 0, b)),
        ],
        out_specs=pl.BlockSpec((ATTN_TQ, Q_PER_KV * HEAD_DIM), lambda b, h, i: (b * qt + i, h)),
        out_shape=jax.ShapeDtypeStruct((t_tokens, ATTN_Q_DIM), BF16),
        scratch_shapes=[pltpu.VMEM((2, ATTN_TK, Q_PER_KV * ATTN_SUB), F32)],
        compiler_params=_cparams(("arbitrary", "arbitrary", "arbitrary")),
        name="attention",
    )(q, k, vt)


def _fourier_tables():
    r = FFT_RADIX
    n = FOURIER_GROUP_DIM
    ch = np.arange(n)
    ang_c = 2.0 * np.pi * np.outer(ch, ch) / n
    scale = 1.0 / math.sqrt(SEQ * n)
    w_ch = np.concatenate([np.cos(ang_c), -np.sin(ang_c)], axis=1) * scale
    a = np.arange(r)
    ang1 = 2.0 * np.pi * np.outer(a, a) / r
    c1, s1 = np.cos(ang1), np.sin(ang1)
    w1 = np.zeros((r, 2, 2, r))
    w1[:, 0, 0, :] = c1
    w1[:, 0, 1, :] = s1
    w1[:, 1, 0, :] = -s1
    w1[:, 1, 1, :] = c1
    w1 = w1.reshape(2 * r, 2 * r)
    sp = (np.arange(r)[:, None] + r * np.arange(r)[None, :])
    th = 2.0 * np.pi * sp[:, :, None] * np.arange(r)[None, None, :] / SEQ
    w2 = np.concatenate([np.cos(th), np.sin(th)], axis=2)
    return tuple(jnp.asarray(t, dtype=F32).astype(BF16) for t in (w_ch, w1, w2))


def _fourier_kernel(f_ref, wch_ref, w1_ref, w2_ref, o_ref, x_scr, y_scr, o_scr):
    r = FFT_RADIX
    n = FOURIER_GROUP_DIM
    z = jnp.dot(f_ref[...], wch_ref[...], preferred_element_type=F32)
    x_scr[0:r] = z[:, :n].astype(BF16).reshape(r, r, n)
    x_scr[r:2 * r] = z[:, n:].astype(BF16).reshape(r, r, n)
    y = lax.dot_general(w1_ref[...], x_scr[...], (((1,), (0,)), ((), ())),
                        preferred_element_type=F32)
    y_scr[...] = y.astype(BF16)

    def body(c, carry):
        slab = y_scr[pl.ds(2 * c, 2)].reshape(2 * r, n)
        res = jnp.dot(w2_ref[c], slab, preferred_element_type=F32)
        for s in range(n // LANES):
            o_scr[s, pl.ds(c, r, stride=r), :] = res[:, s * LANES:(s + 1) * LANES]
        return carry

    lax.fori_loop(0, r, body, 0)
    for s in range(n // LANES):
        o_ref[:, s * LANES:(s + 1) * LANES] = o_scr[s].astype(BF16)


def _fourier(f):
    t_tokens = f.shape[1]
    n_b = t_tokens // SEQ
    r = FFT_RADIX
    n = FOURIER_GROUP_DIM
    w_ch, w1, w2 = _fourier_tables()
    blk = pl.BlockSpec((None, SEQ, n), lambda g, b: (g, b, 0))
    return pl.pallas_call(
        _fourier_kernel,
        grid=(N_FOURIER_GROUPS, n_b),
        in_specs=[
            blk,
            pl.BlockSpec((n, 2 * n), lambda g, b: (0, 0)),
            pl.BlockSpec((2 * r, 2 * r), lambda g, b: (0, 0)),
            pl.BlockSpec((r, r, 2 * r), lambda g, b: (0, 0, 0)),
        ],
        out_specs=blk,
        out_shape=jax.ShapeDtypeStruct((N_FOURIER_GROUPS, t_tokens, n), BF16),
        scratch_shapes=[
            pltpu.VMEM((2 * r, r, n), BF16),
            pltpu.VMEM((2 * r, r, n), BF16),
            pltpu.VMEM((n // LANES, SEQ, LANES), F32),
        ],
        compiler_params=_cparams(("arbitrary", "arbitrary")),
        name="fourier",
    )(f, w_ch, w1, w2)


def _post_mix_kernel(attn_ref, four_ref, gate_ref, x_ref, mod_ref, gpost_ref, gpre_ref,
                     wao_ref, wfo_ref, wmo_ref, wr_ref, br_ref,
                     x1_ref, h2_ref, logit_ref):
    a = jnp.dot(attn_ref[...], wao_ref[...], preferred_element_type=F32)
    fo = jnp.dot(four_ref[0], wfo_ref[0], preferred_element_type=F32)
    for g in range(1, N_FOURIER_GROUPS):
        fo = fo + jnp.dot(four_ref[g], wfo_ref[g], preferred_element_type=F32)
    ga = gate_ref[:, :D_MODEL].astype(F32)
    gf = gate_ref[:, D_MODEL:].astype(F32)
    m = (ga * a + gf * fo).astype(BF16)
    mixed = jnp.dot(m, wmo_ref[...], preferred_element_type=F32)
    ms = jnp.mean(mixed * mixed, axis=-1, keepdims=True)
    y = mixed * lax.rsqrt(ms + NORM_EPS) * gpost_ref[...]
    x1 = x_ref[...] + mod_ref[2:3, :] * y
    x1_ref[...] = x1
    ms1 = jnp.mean(x1 * x1, axis=-1, keepdims=True)
    h2 = (x1 * lax.rsqrt(ms1 + NORM_EPS) * gpre_ref[...]) * (1.0 + mod_ref[4:5, :]) + mod_ref[3:4, :]
    h2_ref[...] = h2
    logit_ref[...] = jnp.dot(h2.astype(BF16), wr_ref[...], preferred_element_type=F32) + br_ref[...]


def _post_mix(attn, four, gates, x2, mod3, g_post, g_pre, wao, wfo4, wmo, w_router, b_router):
    t_tokens = x2.shape[0]
    tm = POST_TM
    tiles_per_seq = SEQ // tm
    n = FOURIER_GROUP_DIM
    const2 = lambda i: (0, 0)
    resident = dict(pipeline_mode=pl.Buffered(1))
    return pl.pallas_call(
        _post_mix_kernel,
        grid=(t_tokens // tm,),
        in_specs=[
            pl.BlockSpec((tm, ATTN_Q_DIM), lambda i: (i, 0)),
            pl.BlockSpec((N_FOURIER_GROUPS, tm, n), lambda i: (0, i, 0)),
            pl.BlockSpec((tm, 2 * D_MODEL), lambda i: (i, 0)),
            pl.BlockSpec((tm, D_MODEL), lambda i: (i, 0)),
            pl.BlockSpec((None, 6, D_MODEL), lambda i: (i // tiles_per_seq, 0, 0)),
            pl.BlockSpec((1, D_MODEL), const2),
            pl.BlockSpec((1, D_MODEL), const2),
            pl.BlockSpec((ATTN_Q_DIM, D_MODEL), const2, **resident),
            pl.BlockSpec((N_FOURIER_GROUPS, n, D_MODEL), lambda i: (0, 0, 0), **resident),
            pl.BlockSpec((D_MODEL, D_MODEL), const2, **resident),
            pl.BlockSpec((D_MODEL, LANES), const2, **resident),
            pl.BlockSpec((1, LANES), const2),
        ],
        out_specs=[
            pl.BlockSpec((tm, D_MODEL), lambda i: (i, 0)),
            pl.BlockSpec((tm, D_MODEL), lambda i: (i, 0)),
            pl.BlockSpec((tm, LANES), lambda i: (i, 0)),
        ],
        out_shape=[
            jax.ShapeDtypeStruct((t_tokens, D_MODEL), F32),
            jax.ShapeDtypeStruct((t_tokens, D_MODEL), F32),
            jax.ShapeDtypeStruct((t_tokens, LANES), F32),
        ],
        compiler_params=_cparams(("arbitrary",)),
        name="post_mix",
    )(attn, four, gates, x2, mod3, g_post.reshape(1, D_MODEL), g_pre.reshape(1, D_MODEL),
      wao, wfo4, wmo, w_router, b_router)


def _first_argmax(vals, lane):
    mx = jnp.max(vals, axis=-1, keepdims=True)
    idx = jnp.min(jnp.where(vals == mx, lane, float(LANES)), axis=-1, keepdims=True)
    return mx, idx


def _route_kernel(logit_ref, tri_ref, out_ref, cnt_ref, carry_scr):
    i = pl.program_id(0)

    @pl.when(i == 0)
    def _():
        carry_scr[...] = jnp.zeros_like(carry_scr)

    lg = logit_ref[...]
    lane = lax.broadcasted_iota(jnp.int32, lg.shape, 1).astype(F32)
    neg = jnp.float32(-jnp.inf)
    gl = jnp.where(lane < N_GROUPS, lg, neg)
    gmax, gidx = _first_argmax(gl, lane)
    g_w = 1.0 / jnp.sum(jnp.exp(gl - gmax), axis=-1, keepdims=True)
    lo = N_GROUPS + EXPERTS_PER_GROUP * gidx
    el = jnp.where((lane >= lo) & (lane < lo + EXPERTS_PER_GROUP), lg, neg)
    m1, i1 = _first_argmax(el, lane)
    el2 = jnp.where(lane == i1, neg, el)
    m2, i2 = _first_argmax(el2, lane)
    p2 = jnp.exp(m2 - m1)
    w1 = g_w / (1.0 + p2)
    w2 = g_w * p2 / (1.0 + p2)
    oh = jnp.where((lane == i1) | (lane == i2), 1.0, 0.0)
    before = jnp.dot(tri_ref[...], oh.astype(BF16), preferred_element_type=F32) + carry_scr[...]
    r1 = jnp.sum(jnp.where(lane == i1, before, 0.0), axis=-1, keepdims=True)
    r2 = jnp.sum(jnp.where(lane == i2, before, 0.0), axis=-1, keepdims=True)
    carry_scr[...] = carry_scr[...] + jnp.sum(oh, axis=0, keepdims=True)
    e1 = i1 - N_GROUPS
    e2 = i2 - N_GROUPS
    out = jnp.where(lane == 0, e1, 0.0)
    out = jnp.where(lane == 1, e2, out)
    out = jnp.where(lane == 2, r1, out)
    out = jnp.where(lane == 3, r2, out)
    out = jnp.where(lane == 4, w1, out)
    out = jnp.where(lane == 5, w2, out)
    out_ref[...] = out
    cnt_ref[...] = jnp.broadcast_to(carry_scr[...], cnt_ref.shape)


def _route(logits):
    t_tokens = logits.shape[0]
    tb = ROUTE_TB
    tri = jnp.asarray(np.tril(np.ones((tb, tb), np.float32), -1), dtype=BF16)
    return pl.pallas_call(
        _route_kernel,
        grid=(t_tokens // tb,),
        in_specs=[
            pl.BlockSpec((tb, LANES), lambda i: (i, 0)),
            pl.BlockSpec((tb, tb), lambda i: (0, 0)),
        ],
        out_specs=[
            pl.BlockSpec((tb, LANES), lambda i: (i, 0)),
            pl.BlockSpec((8, LANES), lambda i: (0, 0)),
        ],
        out_shape=[
            jax.ShapeDtypeStruct((t_tokens, LANES), F32),
            jax.ShapeDtypeStruct((8, LANES), F32),
        ],
        scratch_shapes=[pltpu.VMEM((1, LANES), F32)],
        compiler_params=_cparams(("arbitrary",)),
        name="route",
    )(logits, tri)


def _gather_row_copy(src_hbm, tok_ref, j, buf, sem):
    return pltpu.make_async_copy(src_hbm.at[pl.ds(tok_ref[0, 0, j], 1), :], buf.at[pl.ds(j, 1), :], sem)


def _dispatch_kernel(tok_ref, h_hbm, xs_ref, buf, sem):
    def start(j, carry):
        _gather_row_copy(h_hbm, tok_ref, j, buf, sem).start()
        return carry

    lax.fori_loop(0, MOE_TM, start, 0)

    def wait(j, carry):
        _gather_row_copy(h_hbm, tok_ref, j, buf, sem).wait()
        return carry

    lax.fori_loop(0, MOE_TM, wait, 0)
    xs_ref[...] = buf[...].astype(BF16)


def _dispatch(tok_sorted, h2):
    n_rows = tok_sorted.shape[0]
    nb = n_rows // MOE_TM
    return pl.pallas_call(
        _dispatch_kernel,
        grid=(nb,),
        in_specs=[
            pl.BlockSpec((1, 1, MOE_TM), lambda i: (i, 0, 0), memory_space=pltpu.SMEM),
            pl.BlockSpec(memory_space=pl.ANY),
        ],
        out_specs=pl.BlockSpec((MOE_TM, D_MODEL), lambda i: (i, 0)),
        out_shape=jax.ShapeDtypeStruct((n_rows, D_MODEL), BF16),
        scratch_shapes=[pltpu.VMEM((MOE_TM, D_MODEL), F32), pltpu.SemaphoreType.DMA(())],
        compiler_params=_cparams(("arbitrary",)),
        name="dispatch",
    )(tok_sorted.reshape(nb, 1, MOE_TM), h2)


def _expert_kernel(tile_ref, exp_ref, start_ref, end_ref, nitem_ref,
                   xs_ref, wg_ref, wu_ref, wd_ref, ys_ref):
    k = pl.program_id(0)
    tile = tile_ref[k]
    e = exp_ref[k]
    prev_tile = tile_ref[jnp.maximum(k - 1, 0)]
    first_visit = (k == 0) | (tile != prev_tile)

    @pl.when(first_visit)
    def _():
        ys_ref[...] = jnp.zeros_like(ys_ref)

    @pl.when(k < nitem_ref[0])
    def _():
        x = xs_ref[...]
        g = jnp.dot(x, wg_ref[...].astype(BF16), preferred_element_type=F32)
        u = jnp.dot(x, wu_ref[...].astype(BF16), preferred_element_type=F32)
        hmid = (g * jax.nn.sigmoid(g) * u).astype(BF16)
        y = jnp.dot(hmid, wd_ref[...].astype(BF16), preferred_element_type=F32)
        row = tile * MOE_TM + lax.broadcasted_iota(jnp.int32, (MOE_TM, 1), 0)
        valid = (row >= start_ref[e]) & (row < end_ref[e])
        ys_ref[...] += jnp.where(valid, y, 0.0)


def _experts(item_tile, item_exp, starts, ends, n_items, xs, wg, wu, wd):
    n_rows = xs.shape[0]
    n_tiles = n_rows // MOE_TM
    max_items = n_tiles + N_EXPERTS - 1
    grid_spec = pltpu.PrefetchScalarGridSpec(
        num_scalar_prefetch=5,
        grid=(max_items,),
        in_specs=[
            pl.BlockSpec((MOE_TM, D_MODEL), lambda k, t, e, s, en, n: (t[k], 0)),
            pl.BlockSpec((None, D_MODEL, EXPERT_FF), lambda k, t, e, s, en, n: (e[k], 0, 0)),
            pl.BlockSpec((None, D_MODEL, EXPERT_FF), lambda k, t, e, s, en, n: (e[k], 0, 0)),
            pl.BlockSpec((None, EXPERT_FF, D_MODEL), lambda k, t, e, s, en, n: (e[k], 0, 0)),
        ],
        out_specs=pl.BlockSpec((MOE_TM, D_MODEL), lambda k, t, e, s, en, n: (t[k], 0)),
    )
    return pl.pallas_call(
        _expert_kernel,
        grid_spec=grid_spec,
        out_shape=jax.ShapeDtypeStruct((n_rows, D_MODEL), F32),
        compiler_params=_cparams(("arbitrary",)),
        name="experts",
    )(item_tile, item_exp, starts, ends, n_items, xs, wg, wu, wd)


def _expert_items(counts):
    ends = jnp.cumsum(counts)
    starts = ends - counts
    first_tile = starts // MOE_TM
    last_tile = (ends - 1) // MOE_TM
    n_it = jnp.where(counts > 0, last_tile - first_tile + 1, 0)
    it_end = jnp.cumsum(n_it)
    it_start = it_end - n_it
    total = it_end[-1]
    return starts, ends, first_tile, it_start, it_end, total


def _combine_kernel(p1_ref, p2_ref, ys_hbm, route_ref, x1_ref, mod_ref, g_ref, o_ref, y1_scr, y2_scr, sem):
    def start(j, carry):
        pltpu.make_async_copy(ys_hbm.at[pl.ds(p1_ref[0, 0, j], 1), :], y1_scr.at[pl.ds(j, 1), :], sem).start()
        pltpu.make_async_copy(ys_hbm.at[pl.ds(p2_ref[0, 0, j], 1), :], y2_scr.at[pl.ds(j, 1), :], sem).start()
        return carry

    lax.fori_loop(0, MOE_TB, start, 0)

    def wait(j, carry):
        pltpu.make_async_copy(ys_hbm.at[pl.ds(p1_ref[0, 0, j], 1), :], y1_scr.at[pl.ds(j, 1), :], sem).wait()
        pltpu.make_async_copy(ys_hbm.at[pl.ds(p2_ref[0, 0, j], 1), :], y2_scr.at[pl.ds(j, 1), :], sem).wait()
        return carry

    lax.fori_loop(0, MOE_TB, wait, 0)
    w1 = route_ref[:, 4:5]
    w2 = route_ref[:, 5:6]
    ffn = w1 * y1_scr[...] + w2 * y2_scr[...]
    ms = jnp.mean(ffn * ffn, axis=-1, keepdims=True)
    y = ffn * lax.rsqrt(ms + NORM_EPS) * g_ref[...]
    o_ref[...] = x1_ref[...] + mod_ref[5:6, :] * y


def _combine(pos1, pos2, ys, route, x1, mod3, g_post):
    t_tokens = x1.shape[0]
    tb = MOE_TB
    nb = t_tokens // tb
    tiles_per_seq = SEQ // tb
    smem_blk = pl.BlockSpec((1, 1, tb), lambda i: (i, 0, 0), memory_space=pltpu.SMEM)
    return pl.pallas_call(
        _combine_kernel,
        grid=(nb,),
        in_specs=[
            smem_blk, smem_blk,
            pl.BlockSpec(memory_space=pl.ANY),
            pl.BlockSpec((tb, LANES), lambda i: (i, 0)),
            pl.BlockSpec((tb, D_MODEL), lambda i: (i, 0)),
            pl.BlockSpec((None, 6, D_MODEL), lambda i: (i // tiles_per_seq, 0, 0)),
            pl.BlockSpec((1, D_MODEL), lambda i: (0, 0)),
        ],
        out_specs=pl.BlockSpec((tb, D_MODEL), lambda i: (i, 0)),
        out_shape=jax.ShapeDtypeStruct((t_tokens, D_MODEL), F32),
        scratch_shapes=[
            pltpu.VMEM((tb, D_MODEL), F32),
            pltpu.VMEM((tb, D_MODEL), F32),
            pltpu.SemaphoreType.DMA(()),
        ],
        compiler_params=_cparams(("arbitrary",)),
        name="combine",
    )(pos1.reshape(nb, 1, tb), pos2.reshape(nb, 1, tb), ys, route, x1, mod3, g_post.reshape(1, D_MODEL))


def _rope_tables():
    rows = SEQ // GRID_W
    row = jnp.repeat(jnp.arange(rows, dtype=F32), GRID_W)
    col = jnp.tile(jnp.arange(GRID_W, dtype=F32), rows)
    n_freq = HEAD_DIM // 4
    inv = ROPE_THETA ** (-jnp.arange(n_freq, dtype=F32) / n_freq)
    ang_r = row[:, None] * inv
    ang_c = col[:, None] * inv
    ang = jnp.concatenate([ang_r, ang_r, ang_c, ang_c], axis=-1)
    cos, sin = jnp.cos(ang), jnp.sin(ang)
    first = (jnp.arange(HEAD_DIM) % (2 * n_freq)) < n_freq
    sin_a = jnp.where(first, -sin, 0.0)
    sin_b = jnp.where(first, 0.0, sin)
    return cos, sin_a, sin_b


def kernel(x, c, w_ada, b_ada, g_pre_mix, g_post_mix, w_in, q_norm_g, k_norm_g, w_attn_out, w_fourier_out,
           w_mix_out, g_pre_ffn, g_post_ffn, w_group_router, b_group_router, w_expert_router, b_expert_router,
           w_exp_gate, w_exp_up, w_exp_down):
    n_b, seq, d = x.shape
    assert seq == SEQ and d == D_MODEL
    t_tokens = n_b * seq
    x2 = x.reshape(t_tokens, d)

    c_pad = jnp.zeros((8, d), F32).at[:n_b].set(c)
    mod = _ada(c_pad, w_ada, b_ada)[:n_b]
    mod3 = mod.reshape(n_b, 6, d)

    cos, sin_a, sin_b = _rope_tables()
    q, k, v, f, gates = _in_proj(x2, mod3, g_pre_mix, w_in.astype(BF16), cos, sin_a, sin_b, q_norm_g, k_norm_g)
    attn = _attention(q, k, v)
    four = _fourier(f)

    n_r = N_GROUPS + N_EXPERTS
    w_router = jnp.zeros((d, LANES), F32).at[:, :N_GROUPS].set(w_group_router).at[:, N_GROUPS:n_r].set(w_expert_router)
    b_router = jnp.zeros((1, LANES), F32).at[0, :N_GROUPS].set(b_group_router).at[0, N_GROUPS:n_r].set(b_expert_router)
    x1, h2, logits = _post_mix(
        attn, four, gates, x2, mod3, g_post_mix, g_pre_ffn,
        w_attn_out.astype(BF16),
        w_fourier_out.astype(BF16).reshape(N_FOURIER_GROUPS, FOURIER_GROUP_DIM, d),
        w_mix_out.astype(BF16), w_router.astype(BF16), b_router)

    route, cnt = _route(logits)
    counts = cnt[0, N_GROUPS:n_r].astype(jnp.int32)
    e1 = route[:, 0].astype(jnp.int32)
    e2 = route[:, 1].astype(jnp.int32)
    starts, ends, first_tile, it_start, it_end, total = _expert_items(counts)
    pos1 = starts[e1] + route[:, 2].astype(jnp.int32)
    pos2 = starts[e2] + route[:, 3].astype(jnp.int32)

    tok = jnp.arange(t_tokens, dtype=jnp.int32)
    tok_sorted = (jnp.zeros((2 * t_tokens,), jnp.int32)
                  .at[pos1].set(tok, unique_indices=True)
                  .at[pos2].set(tok, unique_indices=True))
    xs = _dispatch(tok_sorted, h2)

    n_tiles = 2 * t_tokens // MOE_TM
    max_items = n_tiles + N_EXPERTS - 1
    kk = jnp.minimum(jnp.arange(max_items, dtype=jnp.int32), total - 1)
    item_exp = jnp.sum((it_end[None, :] <= kk[:, None]).astype(jnp.int32), axis=1)
    item_tile = first_tile[item_exp] + kk - it_start[item_exp]
    ys = _experts(item_tile.astype(jnp.int32), item_exp.astype(jnp.int32), starts.astype(jnp.int32),
                  ends.astype(jnp.int32), total.reshape(1).astype(jnp.int32), xs, w_exp_gate, w_exp_up, w_exp_down)

    out = _combine(pos1, pos2, ys, route, x1, mod3, g_post_ffn)
    return out.reshape(n_b, seq, d)
```

```python
import math

import numpy as np
import jax
import jax.numpy as jnp
from jax import lax
from jax.experimental import pallas as pl
from jax.experimental.pallas import tpu as pltpu

F32 = jnp.float32
BF16 = jnp.bfloat16

D_MODEL = 2048
SEQ = 4096
N_HEADS = 16
N_KV_HEADS = 4
Q_PER_KV = N_HEADS // N_KV_HEADS
HEAD_DIM = 128
ROPE_THETA = 10000.0
GRID_W = 64
ATTN_Q_DIM = N_HEADS * HEAD_DIM
ATTN_KV_DIM = N_KV_HEADS * HEAD_DIM
N_FOURIER_GROUPS = 4
FOURIER_GROUP_DIM = 256
FOURIER_DIM = N_FOURIER_GROUPS * FOURIER_GROUP_DIM
IN_PROJ_DIM = ATTN_Q_DIM + 2 * ATTN_KV_DIM + FOURIER_DIM + 2 * D_MODEL
N_GROUPS = 4
EXPERTS_PER_GROUP = 8
N_EXPERTS = N_GROUPS * EXPERTS_PER_GROUP
EXPERT_FF = 512
NORM_EPS = 1e-6

LANES = 128
FFT_RADIX = 64
FFT_STAGE2_UNROLL = 8
VMEM_LIMIT = 56 * 1024 * 1024

Q_PRESCALE = HEAD_DIM ** -0.5 * math.log2(math.e)

ADA_TN = 1024
INPROJ_TM = 512
INPROJ_GROUPS = N_KV_HEADS
INPROJ_Q_COLS = ATTN_Q_DIM // INPROJ_GROUPS
INPROJ_GATE_COLS = 2 * D_MODEL // INPROJ_GROUPS
INPROJ_GATE_SPLIT = 512
VT_ROWS = HEAD_DIM + 16
ATTN_TQ = 256
ATTN_SUB = 128
ATTN_TK = 512
POST_TM = 256
ROUTE_TB = 512
MOE_TB = 256
MOE_TM = 256


def _cparams(sem, vmem=VMEM_LIMIT):
    return pltpu.CompilerParams(dimension_semantics=sem, vmem_limit_bytes=vmem)


def _ada_kernel(c_ref, w_ref, b_ref, o_ref):
    cs = c_ref[...]
    s = cs * jax.nn.sigmoid(cs)
    o_ref[...] = jnp.dot(s.astype(BF16), w_ref[...].astype(BF16), preferred_element_type=F32) + b_ref[...]


def _ada(c_pad, w_ada, b_ada):
    n = w_ada.shape[1]
    return pl.pallas_call(
        _ada_kernel,
        grid=(n // ADA_TN,),
        in_specs=[
            pl.BlockSpec((8, D_MODEL), lambda j: (0, 0)),
            pl.BlockSpec((D_MODEL, ADA_TN), lambda j: (0, j)),
            pl.BlockSpec((1, ADA_TN), lambda j: (0, j)),
        ],
        out_specs=pl.BlockSpec((8, ADA_TN), lambda j: (0, j)),
        out_shape=jax.ShapeDtypeStruct((8, n), F32),
        compiler_params=_cparams(("arbitrary",)),
        name="ada",
    )(c_pad, w_ada, b_ada.reshape(1, n))


def _head_norm_rope(t, g, cos, sin_a, sin_b):
    ms = jnp.mean(t * t, axis=-1, keepdims=True)
    y = t * lax.rsqrt(ms + NORM_EPS) * g
    return y * cos + pltpu.roll(y, 96, 1) * sin_a + pltpu.roll(y, 32, 1) * sin_b


def _inproj_kernel(x_ref, mod_ref, g_ref, w_ref, cos_ref, sa_ref, sb_ref, qg_ref, kg_ref,
                   q_ref, k_ref, v_ref, f_ref, gate_ref, h_scr):
    j = pl.program_id(1)

    @pl.when(j == 0)
    def _():
        x = x_ref[...]
        ms = jnp.mean(x * x, axis=-1, keepdims=True)
        y = x * lax.rsqrt(ms + NORM_EPS) * g_ref[...]
        h_scr[...] = (y * (1.0 + mod_ref[1:2, :]) + mod_ref[0:1, :]).astype(BF16)

    h = h_scr[...]
    cos, sin_a, sin_b = cos_ref[...], sa_ref[...], sb_ref[...]
    c0 = 0
    r_q = jnp.dot(h, w_ref[:, c0:c0 + INPROJ_Q_COLS], preferred_element_type=F32)
    for hh in range(INPROJ_Q_COLS // HEAD_DIM):
        t = _head_norm_rope(r_q[:, hh * HEAD_DIM:(hh + 1) * HEAD_DIM], qg_ref[...], cos, sin_a, sin_b)
        q_ref[hh] = (t * Q_PRESCALE).astype(BF16)
    c0 += INPROJ_Q_COLS
    r_kv = jnp.dot(h, w_ref[:, c0:c0 + 2 * HEAD_DIM], preferred_element_type=F32)
    k_ref[0] = _head_norm_rope(r_kv[:, :HEAD_DIM], kg_ref[...], cos, sin_a, sin_b).astype(BF16)
    v_ref[0, 0:HEAD_DIM, :] = r_kv[:, HEAD_DIM:].T.astype(BF16)
    ones_row = lax.broadcasted_iota(jnp.int32, (VT_ROWS - HEAD_DIM, INPROJ_TM), 0) == 0
    v_ref[0, HEAD_DIM:VT_ROWS, :] = jnp.where(ones_row, 1.0, 0.0).astype(BF16)
    c0 += 2 * HEAD_DIM
    f_ref[0] = jnp.dot(h, w_ref[:, c0:c0 + FOURIER_GROUP_DIM], preferred_element_type=F32).astype(BF16)
    c0 += FOURIER_GROUP_DIM
    for s in range(INPROJ_GATE_COLS // INPROJ_GATE_SPLIT):
        r_g = jnp.dot(h, w_ref[:, c0:c0 + INPROJ_GATE_SPLIT], preferred_element_type=F32)
        gate_ref[:, s * INPROJ_GATE_SPLIT:(s + 1) * INPROJ_GATE_SPLIT] = jax.nn.sigmoid(r_g).astype(BF16)
        c0 += INPROJ_GATE_SPLIT


def _regroup_w_in(w_in):
    o1 = ATTN_Q_DIM
    o2 = o1 + ATTN_KV_DIM
    o3 = o2 + ATTN_KV_DIM
    o4 = o3 + FOURIER_DIM
    parts = []
    for j in range(INPROJ_GROUPS):
        parts += [
            w_in[:, j * INPROJ_Q_COLS:(j + 1) * INPROJ_Q_COLS],
            w_in[:, o1 + j * HEAD_DIM:o1 + (j + 1) * HEAD_DIM],
            w_in[:, o2 + j * HEAD_DIM:o2 + (j + 1) * HEAD_DIM],
            w_in[:, o3 + j * FOURIER_GROUP_DIM:o3 + (j + 1) * FOURIER_GROUP_DIM],
            w_in[:, o4 + j * INPROJ_GATE_COLS:o4 + (j + 1) * INPROJ_GATE_COLS],
        ]
    return jnp.concatenate(parts, axis=1).astype(BF16)


def _in_proj(x2, mod3, g_pre, w_grouped, cos, sin_a, sin_b, qg, kg):
    t_tokens = x2.shape[0]
    tm = INPROJ_TM
    tn = IN_PROJ_DIM // INPROJ_GROUPS
    tiles_per_seq = SEQ // tm
    row_tbl = pl.BlockSpec((tm, HEAD_DIM), lambda i, j: (i % tiles_per_seq, 0))
    vec_hd = pl.BlockSpec((1, HEAD_DIM), lambda i, j: (0, 0))
    return pl.pallas_call(
        _inproj_kernel,
        grid=(t_tokens // tm, INPROJ_GROUPS),
        in_specs=[
            pl.BlockSpec((tm, D_MODEL), lambda i, j: (i, 0)),
            pl.BlockSpec((None, 6, D_MODEL), lambda i, j: (i // tiles_per_seq, 0, 0)),
            pl.BlockSpec((1, D_MODEL), lambda i, j: (0, 0)),
            pl.BlockSpec((D_MODEL, tn), lambda i, j: (0, j)),
            row_tbl, row_tbl, row_tbl, vec_hd, vec_hd,
        ],
        out_specs=[
            pl.BlockSpec((INPROJ_Q_COLS // HEAD_DIM, tm, HEAD_DIM), lambda i, j: (j, i, 0)),
            pl.BlockSpec((1, tm, HEAD_DIM), lambda i, j: (j, i, 0)),
            pl.BlockSpec((1, VT_ROWS, tm), lambda i, j: (j, 0, i)),
            pl.BlockSpec((1, tm, FOURIER_GROUP_DIM), lambda i, j: (j, i, 0)),
            pl.BlockSpec((tm, INPROJ_GATE_COLS), lambda i, j: (i, j)),
        ],
        out_shape=[
            jax.ShapeDtypeStruct((N_HEADS, t_tokens, HEAD_DIM), BF16),
            jax.ShapeDtypeStruct((N_KV_HEADS, t_tokens, HEAD_DIM), BF16),
            jax.ShapeDtypeStruct((N_KV_HEADS, VT_ROWS, t_tokens), BF16),
            jax.ShapeDtypeStruct((N_FOURIER_GROUPS, t_tokens, FOURIER_GROUP_DIM), BF16),
            jax.ShapeDtypeStruct((t_tokens, 2 * D_MODEL), BF16),
        ],
        scratch_shapes=[pltpu.VMEM((tm, D_MODEL), BF16)],
        compiler_params=_cparams(("arbitrary", "arbitrary")),
        name="in_proj",
    )(x2, mod3, g_pre.reshape(1, D_MODEL), w_grouped, cos, sin_a, sin_b,
      qg.reshape(1, HEAD_DIM), kg.reshape(1, HEAD_DIM))


def _attn_kernel(q_ref, k_ref, vt_ref, o_ref, s_scr):
    n_chunks = SEQ // ATTN_TK
    n_sub = ATTN_TQ // ATTN_SUB

    def load_q(sub):
        return q_ref[:, sub * ATTN_SUB:(sub + 1) * ATTN_SUB, :].reshape(Q_PER_KV * ATTN_SUB, HEAD_DIM)

    def scores(q, c, slot):
        k_c = k_ref[c * ATTN_TK:(c + 1) * ATTN_TK, :]
        s_scr[slot] = lax.dot_general(k_c, q, (((1,), (1,)), ((), ())), preferred_element_type=F32)

    q = load_q(0)
    scores(q, 0, 0)
    unit = 0
    for sub in range(n_sub):
        q_next = load_q(sub + 1) if sub + 1 < n_sub else None
        m = acc = None
        for c in range(n_chunks):
            if c + 1 < n_chunks:
                scores(q, c + 1, (unit + 1) % 2)
            elif q_next is not None:
                scores(q_next, 0, (unit + 1) % 2)
            s = s_scr[unit % 2]
            vt_c = vt_ref[:, c * ATTN_TK:(c + 1) * ATTN_TK]
            m_c = jnp.max(s, axis=0, keepdims=True)
            if c == 0:
                m = m_c
                acc = jnp.dot(vt_c, jnp.exp2(s - m).astype(BF16), preferred_element_type=F32)
            else:
                m_new = jnp.maximum(m, m_c)
                alpha = jnp.exp2(m - m_new)
                acc = alpha * acc + jnp.dot(vt_c, jnp.exp2(s - m_new).astype(BF16), preferred_element_type=F32)
                m = m_new
            unit += 1
        o_t = acc[:HEAD_DIM] * (1.0 / acc[HEAD_DIM:HEAD_DIM + 1])
        r0 = sub * ATTN_SUB
        for hh in range(Q_PER_KV):
            o_ref[r0:r0 + ATTN_SUB, hh * HEAD_DIM:(hh + 1) * HEAD_DIM] = (
                o_t[:, hh * ATTN_SUB:(hh + 1) * ATTN_SUB].T.astype(BF16))
        q = q_next


def _attention(q, k, vt):
    t_tokens = q.shape[1]
    n_b = t_tokens // SEQ
    qt = SEQ // ATTN_TQ
    return pl.pallas_call(
        _attn_kernel,
        grid=(n_b, N_KV_HEADS, qt),
        in_specs=[
            pl.BlockSpec((Q_PER_KV, ATTN_TQ, HEAD_DIM), lambda b, h, i: (h, b * qt + i, 0)),
            pl.BlockSpec((None, SEQ, HEAD_DIM), lambda b, h, i: (h, b, 0)),
            pl.BlockSpec((None, VT_ROWS, SEQ), lambda b, h, i: (h, 0, b)),
        ],
        out_specs=pl.BlockSpec((ATTN_TQ, Q_PER_KV * HEAD_DIM), lambda b, h, i: (b * qt + i, h)),
        out_shape=jax.ShapeDtypeStruct((t_tokens, ATTN_Q_DIM), BF16),
        scratch_shapes=[pltpu.VMEM((2, ATTN_TK, Q_PER_KV * ATTN_SUB), F32)],
        compiler_params=_cparams(("arbitrary", "arbitrary", "arbitrary")),
        name="attention",
    )(q, k, vt)


def _fourier_tables():
    r = FFT_RADIX
    n = FOURIER_GROUP_DIM
    ch = np.arange(n)
    ang_c = 2.0 * np.pi * np.outer(ch, ch) / n
    scale = 1.0 / math.sqrt(SEQ * n)
    w_ch = np.concatenate([np.cos(ang_c), -np.sin(ang_c)], axis=1) * scale
    a = np.arange(r)
    ang1 = 2.0 * np.pi * np.outer(a, a) / r
    c1, s1 = np.cos(ang1), np.sin(ang1)
    w1 = np.zeros((r, 2, 2, r))
    w1[:, 0, 0, :] = c1
    w1[:, 0, 1, :] = s1
    w1[:, 1, 0, :] = -s1
    w1[:, 1, 1, :] = c1
    w1 = w1.reshape(2 * r, 2 * r)
    sp = (np.arange(r)[:, None] + r * np.arange(r)[None, :])
    th = 2.0 * np.pi * sp[:, :, None] * np.arange(r)[None, None, :] / SEQ
    w2 = np.concatenate([np.cos(th), np.sin(th)], axis=2)
    return tuple(jnp.asarray(t, dtype=F32).astype(BF16) for t in (w_ch, w1, w2))


def _fourier_kernel(f_ref, wch_ref, w1_ref, w2_ref, o_ref, x_scr, y_scr, o_scr):
    r = FFT_RADIX
    n = FOURIER_GROUP_DIM
    z = jnp.dot(f_ref[...], wch_ref[...], preferred_element_type=F32)
    x_scr[0:r] = z[:, :n].astype(BF16).reshape(r, r, n)
    x_scr[r:2 * r] = z[:, n:].astype(BF16).reshape(r, r, n)
    y = lax.dot_general(w1_ref[...], x_scr[...], (((1,), (0,)), ((), ())),
                        preferred_element_type=F32)
    y_scr[...] = y.astype(BF16)

    def body(c, carry):
        slab = y_scr[pl.ds(2 * c, 2)].reshape(2 * r, n)
        res = jnp.dot(w2_ref[c], slab, preferred_element_type=F32)
        for s in range(n // LANES):
            o_scr[s, pl.ds(c, r, stride=r), :] = res[:, s * LANES:(s + 1) * LANES]
        return carry

    lax.fori_loop(0, r, body, 0, unroll=FFT_STAGE2_UNROLL)
    for s in range(n // LANES):
        o_ref[:, s * LANES:(s + 1) * LANES] = o_scr[s].astype(BF16)


def _fourier(f):
    t_tokens = f.shape[1]
    n_b = t_tokens // SEQ
    r = FFT_RADIX
    n = FOURIER_GROUP_DIM
    w_ch, w1, w2 = _fourier_tables()
    blk = pl.BlockSpec((None, SEQ, n), lambda g, b: (g, b, 0))
    return pl.pallas_call(
        _fourier_kernel,
        grid=(N_FOURIER_GROUPS, n_b),
        in_specs=[
            blk,
            pl.BlockSpec((n, 2 * n), lambda g, b: (0, 0)),
            pl.BlockSpec((2 * r, 2 * r), lambda g, b: (0, 0)),
            pl.BlockSpec((r, r, 2 * r), lambda g, b: (0, 0, 0)),
        ],
        out_specs=blk,
        out_shape=jax.ShapeDtypeStruct((N_FOURIER_GROUPS, t_tokens, n), BF16),
        scratch_shapes=[
            pltpu.VMEM((2 * r, r, n), BF16),
            pltpu.VMEM((2 * r, r, n), BF16),
            pltpu.VMEM((n // LANES, SEQ, LANES), F32),
        ],
        compiler_params=_cparams(("arbitrary", "arbitrary")),
        name="fourier",
    )(f, w_ch, w1, w2)


def _post_mix_kernel(attn_ref, four_ref, gate_ref, x_ref, mod_ref, gpost_ref, gpre_ref,
                     wao_ref, wfo_ref, wmo_ref, wr_ref, br_ref,
                     x1_ref, h2_ref, logit_ref):
    a = jnp.dot(attn_ref[...], wao_ref[...], preferred_element_type=F32)
    fo = jnp.dot(four_ref[0], wfo_ref[0], preferred_element_type=F32)
    for g in range(1, N_FOURIER_GROUPS):
        fo = fo + jnp.dot(four_ref[g], wfo_ref[g], preferred_element_type=F32)
    ga = gate_ref[:, :D_MODEL].astype(F32)
    gf = gate_ref[:, D_MODEL:].astype(F32)
    m = (ga * a + gf * fo).astype(BF16)
    mixed = jnp.dot(m, wmo_ref[...], preferred_element_type=F32)
    ms = jnp.mean(mixed * mixed, axis=-1, keepdims=True)
    y = mixed * lax.rsqrt(ms + NORM_EPS) * gpost_ref[...]
    x1 = x_ref[...] + mod_ref[2:3, :] * y
    x1_ref[...] = x1
    ms1 = jnp.mean(x1 * x1, axis=-1, keepdims=True)
    h2 = (x1 * lax.rsqrt(ms1 + NORM_EPS) * gpre_ref[...]) * (1.0 + mod_ref[4:5, :]) + mod_ref[3:4, :]
    h2_ref[...] = h2
    logit_ref[...] = jnp.dot(h2.astype(BF16), wr_ref[...], preferred_element_type=F32) + br_ref[...]


def _post_mix(attn, four, gates, x2, mod3, g_post, g_pre, wao, wfo4, wmo, w_router, b_router):
    t_tokens = x2.shape[0]
    tm = POST_TM
    tiles_per_seq = SEQ // tm
    n = FOURIER_GROUP_DIM
    const2 = lambda i: (0, 0)
    resident = dict(pipeline_mode=pl.Buffered(1))
    return pl.pallas_call(
        _post_mix_kernel,
        grid=(t_tokens // tm,),
        in_specs=[
            pl.BlockSpec((tm, ATTN_Q_DIM), lambda i: (i, 0)),
            pl.BlockSpec((N_FOURIER_GROUPS, tm, n), lambda i: (0, i, 0)),
            pl.BlockSpec((tm, 2 * D_MODEL), lambda i: (i, 0)),
            pl.BlockSpec((tm, D_MODEL), lambda i: (i, 0)),
            pl.BlockSpec((None, 6, D_MODEL), lambda i: (i // tiles_per_seq, 0, 0)),
            pl.BlockSpec((1, D_MODEL), const2),
            pl.BlockSpec((1, D_MODEL), const2),
            pl.BlockSpec((ATTN_Q_DIM, D_MODEL), const2, **resident),
            pl.BlockSpec((N_FOURIER_GROUPS, n, D_MODEL), lambda i: (0, 0, 0), **resident),
            pl.BlockSpec((D_MODEL, D_MODEL), const2, **resident),
            pl.BlockSpec((D_MODEL, LANES), const2, **resident),
            pl.BlockSpec((1, LANES), const2),
        ],
        out_specs=[
            pl.BlockSpec((tm, D_MODEL), lambda i: (i, 0)),
            pl.BlockSpec((tm, D_MODEL), lambda i: (i, 0)),
            pl.BlockSpec((tm, LANES), lambda i: (i, 0)),
        ],
        out_shape=[
            jax.ShapeDtypeStruct((t_tokens, D_MODEL), F32),
            jax.ShapeDtypeStruct((t_tokens, D_MODEL), F32),
            jax.ShapeDtypeStruct((t_tokens, LANES), F32),
        ],
        compiler_params=_cparams(("arbitrary",)),
        name="post_mix",
    )(attn, four, gates, x2, mod3, g_post.reshape(1, D_MODEL), g_pre.reshape(1, D_MODEL),
      wao, wfo4, wmo, w_router, b_router)


def _first_argmax(vals, lane):
    mx = jnp.max(vals, axis=-1, keepdims=True)
    idx = jnp.min(jnp.where(vals == mx, lane, float(LANES)), axis=-1, keepdims=True)
    return mx, idx


def _route_kernel(logit_ref, tri_ref, out_ref, cnt_ref, carry_scr, start_scr):
    phase = pl.program_id(0)
    i = pl.program_id(1)

    @pl.when((phase == 0) & (i == 0))
    def _():
        carry_scr[...] = jnp.zeros_like(carry_scr)
        start_scr[...] = jnp.zeros_like(start_scr)

    @pl.when((phase == 1) & (i == 0))
    def _():
        counts = carry_scr[...]
        lane8 = lax.broadcasted_iota(jnp.int32, counts.shape, 1)
        incl = counts
        shift = 1
        while shift < N_EXPERTS:
            incl = incl + jnp.where(lane8 >= N_GROUPS + shift, pltpu.roll(incl, shift, 1), 0.0)
            shift *= 2
        start_scr[...] = incl - counts
        carry_scr[...] = jnp.zeros_like(carry_scr)

    lg = logit_ref[...]
    lane = lax.broadcasted_iota(jnp.int32, lg.shape, 1).astype(F32)
    neg = jnp.float32(-jnp.inf)
    gl = jnp.where(lane < N_GROUPS, lg, neg)
    gmax, gidx = _first_argmax(gl, lane)
    g_w = 1.0 / jnp.sum(jnp.exp(gl - gmax), axis=-1, keepdims=True)
    lo = N_GROUPS + EXPERTS_PER_GROUP * gidx
    el = jnp.where((lane >= lo) & (lane < lo + EXPERTS_PER_GROUP), lg, neg)
    m1, i1 = _first_argmax(el, lane)
    el2 = jnp.where(lane == i1, neg, el)
    m2, i2 = _first_argmax(el2, lane)
    p2 = jnp.exp(m2 - m1)
    w1 = g_w / (1.0 + p2)
    w2 = g_w * p2 / (1.0 + p2)
    oh = jnp.where((lane == i1) | (lane == i2), 1.0, 0.0)
    before = (jnp.dot(tri_ref[...], oh.astype(BF16), preferred_element_type=F32)
              + carry_scr[0:1, :] + start_scr[0:1, :])
    pos1 = jnp.sum(jnp.where(lane == i1, before, 0.0), axis=-1, keepdims=True)
    pos2 = jnp.sum(jnp.where(lane == i2, before, 0.0), axis=-1, keepdims=True)
    carry_scr[...] = carry_scr[...] + jnp.sum(oh, axis=0, keepdims=True)
    out = jnp.where(lane == 0, i1 - N_GROUPS, 0.0)
    out = jnp.where(lane == 1, i2 - N_GROUPS, out)
    out = jnp.where(lane == 2, pos1, out)
    out = jnp.where(lane == 3, pos2, out)
    out = jnp.where(lane == 4, w1, out)
    out = jnp.where(lane == 5, w2, out)
    out_ref[...] = out
    cnt_ref[...] = carry_scr[...]


def _route(logits):
    t_tokens = logits.shape[0]
    tb = ROUTE_TB
    tri = jnp.asarray(np.tril(np.ones((tb, tb), np.float32), -1), dtype=BF16)
    return pl.pallas_call(
        _route_kernel,
        grid=(2, t_tokens // tb),
        in_specs=[
            pl.BlockSpec((tb, LANES), lambda p, i: (i, 0)),
            pl.BlockSpec((tb, tb), lambda p, i: (0, 0)),
        ],
        out_specs=[
            pl.BlockSpec((tb, LANES), lambda p, i: (i * p, 0)),
            pl.BlockSpec((8, LANES), lambda p, i: (0, 0)),
        ],
        out_shape=[
            jax.ShapeDtypeStruct((t_tokens, LANES), F32),
            jax.ShapeDtypeStruct((8, LANES), F32),
        ],
        scratch_shapes=[pltpu.VMEM((8, LANES), F32), pltpu.VMEM((8, LANES), F32)],
        compiler_params=_cparams(("arbitrary", "arbitrary")),
        name="route",
    )(logits, tri)


def _gather_row_copy(src_hbm, tok_ref, j, buf, sem):
    return pltpu.make_async_copy(src_hbm.at[pl.ds(tok_ref[0, 0, j], 1), :], buf.at[pl.ds(j, 1), :], sem)


def _dispatch_kernel(tok_ref, h_hbm, xs_ref, buf, sem):
    def start(j, carry):
        _gather_row_copy(h_hbm, tok_ref, j, buf, sem).start()
        return carry

    lax.fori_loop(0, MOE_TM, start, 0)

    def wait(j, carry):
        _gather_row_copy(h_hbm, tok_ref, j, buf, sem).wait()
        return carry

    lax.fori_loop(0, MOE_TM, wait, 0)
    xs_ref[...] = buf[...].astype(BF16)


def _dispatch(tok_sorted, h2):
    n_rows = tok_sorted.shape[0]
    nb = n_rows // MOE_TM
    return pl.pallas_call(
        _dispatch_kernel,
        grid=(nb,),
        in_specs=[
            pl.BlockSpec((1, 1, MOE_TM), lambda i: (i, 0, 0), memory_space=pltpu.SMEM),
            pl.BlockSpec(memory_space=pl.ANY),
        ],
        out_specs=pl.BlockSpec((MOE_TM, D_MODEL), lambda i: (i, 0)),
        out_shape=jax.ShapeDtypeStruct((n_rows, D_MODEL), BF16),
        scratch_shapes=[pltpu.VMEM((MOE_TM, D_MODEL), F32), pltpu.SemaphoreType.DMA(())],
        compiler_params=_cparams(("arbitrary",)),
        name="dispatch",
    )(tok_sorted.reshape(nb, 1, MOE_TM), h2)


def _expert_kernel(tile_ref, exp_ref, start_ref, end_ref, nitem_ref,
                   xs_ref, wg_ref, wu_ref, wd_ref, ys_ref, wg_scr, wu_scr, wd_scr):
    k = pl.program_id(0)
    tile = tile_ref[k]
    e = exp_ref[k]
    prev = jnp.maximum(k - 1, 0)
    first_visit = (k == 0) | (tile != tile_ref[prev])
    new_expert = (k == 0) | (e != exp_ref[prev])

    @pl.when(new_expert)
    def _():
        wg_scr[...] = wg_ref[...].astype(BF16)
        wu_scr[...] = wu_ref[...].astype(BF16)
        wd_scr[...] = wd_ref[...].astype(BF16)

    @pl.when(first_visit)
    def _():
        ys_ref[...] = jnp.zeros_like(ys_ref)

    @pl.when(k < nitem_ref[0])
    def _():
        x = xs_ref[...]
        g = jnp.dot(x, wg_scr[...], preferred_element_type=F32)
        u = jnp.dot(x, wu_scr[...], preferred_element_type=F32)
        hmid = (g * jax.nn.sigmoid(g) * u).astype(BF16)
        y = jnp.dot(hmid, wd_scr[...], preferred_element_type=F32)
        row = tile * MOE_TM + lax.broadcasted_iota(jnp.int32, (MOE_TM, 1), 0)
        valid = (row >= start_ref[e]) & (row < end_ref[e])
        ys_ref[...] += jnp.where(valid, y, 0.0)


def _experts(item_tile, item_exp, starts, ends, n_items, xs, wg, wu, wd):
    n_rows = xs.shape[0]
    n_tiles = n_rows // MOE_TM
    max_items = n_tiles + N_EXPERTS - 1
    grid_spec = pltpu.PrefetchScalarGridSpec(
        num_scalar_prefetch=5,
        grid=(max_items,),
        in_specs=[
            pl.BlockSpec((MOE_TM, D_MODEL), lambda k, t, e, s, en, n: (t[k], 0)),
            pl.BlockSpec((None, D_MODEL, EXPERT_FF), lambda k, t, e, s, en, n: (e[k], 0, 0)),
            pl.BlockSpec((None, D_MODEL, EXPERT_FF), lambda k, t, e, s, en, n: (e[k], 0, 0)),
            pl.BlockSpec((None, EXPERT_FF, D_MODEL), lambda k, t, e, s, en, n: (e[k], 0, 0)),
        ],
        out_specs=pl.BlockSpec((MOE_TM, D_MODEL), lambda k, t, e, s, en, n: (t[k], 0)),
        scratch_shapes=[
            pltpu.VMEM((D_MODEL, EXPERT_FF), BF16),
            pltpu.VMEM((D_MODEL, EXPERT_FF), BF16),
            pltpu.VMEM((EXPERT_FF, D_MODEL), BF16),
        ],
    )
    return pl.pallas_call(
        _expert_kernel,
        grid_spec=grid_spec,
        out_shape=jax.ShapeDtypeStruct((n_rows, D_MODEL), F32),
        compiler_params=_cparams(("arbitrary",)),
        name="experts",
    )(item_tile, item_exp, starts, ends, n_items, xs, wg, wu, wd)


def _expert_items(counts):
    ends = jnp.cumsum(counts)
    starts = ends - counts
    first_tile = starts // MOE_TM
    last_tile = (ends - 1) // MOE_TM
    n_it = jnp.where(counts > 0, last_tile - first_tile + 1, 0)
    it_end = jnp.cumsum(n_it)
    it_start = it_end - n_it
    total = it_end[-1]
    return starts, ends, first_tile, it_start, it_end, total


def _combine_kernel(p1_ref, p2_ref, ys_hbm, route_ref, x1_ref, mod_ref, g_ref, o_ref, y1_scr, y2_scr, sem):
    def start(j, carry):
        pltpu.make_async_copy(ys_hbm.at[pl.ds(p1_ref[0, 0, j], 1), :], y1_scr.at[pl.ds(j, 1), :], sem).start()
        pltpu.make_async_copy(ys_hbm.at[pl.ds(p2_ref[0, 0, j], 1), :], y2_scr.at[pl.ds(j, 1), :], sem).start()
        return carry

    lax.fori_loop(0, MOE_TB, start, 0)

    def wait(j, carry):
        pltpu.make_async_copy(ys_hbm.at[pl.ds(p1_ref[0, 0, j], 1), :], y1_scr.at[pl.ds(j, 1), :], sem).wait()
        pltpu.make_async_copy(ys_hbm.at[pl.ds(p2_ref[0, 0, j], 1), :], y2_scr.at[pl.ds(j, 1), :], sem).wait()
        return carry

    lax.fori_loop(0, MOE_TB, wait, 0)
    w1 = route_ref[:, 4:5]
    w2 = route_ref[:, 5:6]
    ffn = w1 * y1_scr[...] + w2 * y2_scr[...]
    ms = jnp.mean(ffn * ffn, axis=-1, keepdims=True)
    y = ffn * lax.rsqrt(ms + NORM_EPS) * g_ref[...]
    o_ref[...] = x1_ref[...] + mod_ref[5:6, :] * y


def _combine(pos1, pos2, ys, route, x1, mod3, g_post):
    t_tokens = x1.shape[0]
    tb = MOE_TB
    nb = t_tokens // tb
    tiles_per_seq = SEQ // tb
    smem_blk = pl.BlockSpec((1, 1, tb), lambda i: (i, 0, 0), memory_space=pltpu.SMEM)
    return pl.pallas_call(
        _combine_kernel,
        grid=(nb,),
        in_specs=[
            smem_blk, smem_blk,
            pl.BlockSpec(memory_space=pl.ANY),
            pl.BlockSpec((tb, LANES), lambda i: (i, 0)),
            pl.BlockSpec((tb, D_MODEL), lambda i: (i, 0)),
            pl.BlockSpec((None, 6, D_MODEL), lambda i: (i // tiles_per_seq, 0, 0)),
            pl.BlockSpec((1, D_MODEL), lambda i: (0, 0)),
        ],
        out_specs=pl.BlockSpec((tb, D_MODEL), lambda i: (i, 0)),
        out_shape=jax.ShapeDtypeStruct((t_tokens, D_MODEL), F32),
        scratch_shapes=[
            pltpu.VMEM((tb, D_MODEL), F32),
            pltpu.VMEM((tb, D_MODEL), F32),
            pltpu.SemaphoreType.DMA(()),
        ],
        compiler_params=_cparams(("arbitrary",)),
        name="combine",
    )(pos1.reshape(nb, 1, tb), pos2.reshape(nb, 1, tb), ys, route, x1, mod3, g_post.reshape(1, D_MODEL))


def _rope_tables():
    rows = SEQ // GRID_W
    row = jnp.repeat(jnp.arange(rows, dtype=F32), GRID_W)
    col = jnp.tile(jnp.arange(GRID_W, dtype=F32), rows)
    n_freq = HEAD_DIM // 4
    inv = ROPE_THETA ** (-jnp.arange(n_freq, dtype=F32) / n_freq)
    ang_r = row[:, None] * inv
    ang_c = col[:, None] * inv
    ang = jnp.concatenate([ang_r, ang_r, ang_c, ang_c], axis=-1)
    cos, sin = jnp.cos(ang), jnp.sin(ang)
    first = (jnp.arange(HEAD_DIM) % (2 * n_freq)) < n_freq
    sin_a = jnp.where(first, -sin, 0.0)
    sin_b = jnp.where(first, 0.0, sin)
    return cos, sin_a, sin_b


def kernel(x, c, w_ada, b_ada, g_pre_mix, g_post_mix, w_in, q_norm_g, k_norm_g, w_attn_out, w_fourier_out,
           w_mix_out, g_pre_ffn, g_post_ffn, w_group_router, b_group_router, w_expert_router, b_expert_router,
           w_exp_gate, w_exp_up, w_exp_down):
    n_b, seq, d = x.shape
    assert seq == SEQ and d == D_MODEL
    t_tokens = n_b * seq
    x2 = x.reshape(t_tokens, d)

    c_pad = jnp.zeros((8, d), F32).at[:n_b].set(c)
    mod = _ada(c_pad, w_ada, b_ada)[:n_b]
    mod3 = mod.reshape(n_b, 6, d)

    cos, sin_a, sin_b = _rope_tables()
    q, k, vt, f, gates = _in_proj(x2, mod3, g_pre_mix, _regroup_w_in(w_in), cos, sin_a, sin_b, q_norm_g, k_norm_g)
    attn = _attention(q, k, vt)
    four = _fourier(f)

    n_r = N_GROUPS + N_EXPERTS
    w_router = jnp.zeros((d, LANES), F32).at[:, :N_GROUPS].set(w_group_router).at[:, N_GROUPS:n_r].set(w_expert_router)
    b_router = jnp.zeros((1, LANES), F32).at[0, :N_GROUPS].set(b_group_router).at[0, N_GROUPS:n_r].set(b_expert_router)
    x1, h2, logits = _post_mix(
        attn, four, gates, x2, mod3, g_post_mix, g_pre_ffn,
        w_attn_out.astype(BF16),
        w_fourier_out.astype(BF16).reshape(N_FOURIER_GROUPS, FOURIER_GROUP_DIM, d),
        w_mix_out.astype(BF16), w_router.astype(BF16), b_router)

    route, cnt = _route(logits)
    counts = cnt[0, N_GROUPS:n_r].astype(jnp.int32)
    starts, ends, first_tile, it_start, it_end, total = _expert_items(counts)
    pos1 = route[:, 2].astype(jnp.int32)
    pos2 = route[:, 3].astype(jnp.int32)

    tok = jnp.arange(t_tokens, dtype=jnp.int32)
    tok_sorted = (jnp.zeros((2 * t_tokens,), jnp.int32)
                  .at[pos1].set(tok, unique_indices=True)
                  .at[pos2].set(tok, unique_indices=True))
    xs = _dispatch(tok_sorted, h2)

    n_tiles = 2 * t_tokens // MOE_TM
    max_items = n_tiles + N_EXPERTS - 1
    kk = jnp.minimum(jnp.arange(max_items, dtype=jnp.int32), total - 1)
    item_exp = jnp.sum((it_end[None, :] <= kk[:, None]).astype(jnp.int32), axis=1)
    item_tile = first_tile[item_exp] + kk - it_start[item_exp]
    ys = _experts(item_tile.astype(jnp.int32), item_exp.astype(jnp.int32), starts.astype(jnp.int32),
                  ends.astype(jnp.int32), total.reshape(1).astype(jnp.int32), xs, w_exp_gate, w_exp_up, w_exp_down)

    out = _combine(pos1, pos2, ys, route, x1, mod3, g_post_ffn)
    return out.reshape(n_b, seq, d)
```

```python
import math

import numpy as np
import jax
import jax.numpy as jnp
from jax import lax
from jax.experimental import pallas as pl
from jax.experimental.pallas import tpu as pltpu

F32 = jnp.float32
BF16 = jnp.bfloat16

D_MODEL = 2048
SEQ = 4096
N_HEADS = 16
N_KV_HEADS = 4
Q_PER_KV = N_HEADS // N_KV_HEADS
HEAD_DIM = 128
ROPE_THETA = 10000.0
GRID_W = 64
ATTN_Q_DIM = N_HEADS * HEAD_DIM
ATTN_KV_DIM = N_KV_HEADS * HEAD_DIM
N_FOURIER_GROUPS = 4
FOURIER_GROUP_DIM = 256
FOURIER_DIM = N_FOURIER_GROUPS * FOURIER_GROUP_DIM
IN_PROJ_DIM = ATTN_Q_DIM + 2 * ATTN_KV_DIM + FOURIER_DIM + 2 * D_MODEL
N_GROUPS = 4
EXPERTS_PER_GROUP = 8
N_EXPERTS = N_GROUPS * EXPERTS_PER_GROUP
EXPERT_FF = 512
NORM_EPS = 1e-6

LANES = 128
FFT_RADIX = 64
FFT_STAGE2_UNROLL = 8
VMEM_LIMIT = 56 * 1024 * 1024

Q_PRESCALE = HEAD_DIM ** -0.5 * math.log2(math.e)

ADA_TN = 1024
INPROJ_TM = 512
INPROJ_GROUPS = N_KV_HEADS
INPROJ_Q_COLS = ATTN_Q_DIM // INPROJ_GROUPS
INPROJ_GATE_COLS = 2 * D_MODEL // INPROJ_GROUPS
INPROJ_GATE_SPLIT = 512
VT_ROWS = HEAD_DIM + 16
ATTN_TQ = 256
ATTN_SUB = 128
ATTN_TK = 512
POST_TM = 256
ROUTE_TB = 512
MOE_TB = 256
MOE_TM = 256
GATHER_UNROLL = 8


def _cparams(sem, vmem=VMEM_LIMIT):
    return pltpu.CompilerParams(dimension_semantics=sem, vmem_limit_bytes=vmem)


def _ada_kernel(c_ref, w_ref, b_ref, o_ref):
    cs = c_ref[...]
    s = cs * jax.nn.sigmoid(cs)
    o_ref[...] = jnp.dot(s.astype(BF16), w_ref[...].astype(BF16), preferred_element_type=F32) + b_ref[...]


def _ada(c_pad, w_ada, b_ada):
    n = w_ada.shape[1]
    return pl.pallas_call(
        _ada_kernel,
        grid=(n // ADA_TN,),
        in_specs=[
            pl.BlockSpec((8, D_MODEL), lambda j: (0, 0)),
            pl.BlockSpec((D_MODEL, ADA_TN), lambda j: (0, j)),
            pl.BlockSpec((1, ADA_TN), lambda j: (0, j)),
        ],
        out_specs=pl.BlockSpec((8, ADA_TN), lambda j: (0, j)),
        out_shape=jax.ShapeDtypeStruct((8, n), F32),
        compiler_params=_cparams(("arbitrary",)),
        name="ada",
    )(c_pad, w_ada, b_ada.reshape(1, n))


def _head_norm_rope(t, g, cos, sin_a, sin_b):
    ms = jnp.mean(t * t, axis=-1, keepdims=True)
    y = t * lax.rsqrt(ms + NORM_EPS) * g
    return y * cos + pltpu.roll(y, 96, 1) * sin_a + pltpu.roll(y, 32, 1) * sin_b


def _inproj_kernel(x_ref, mod_ref, g_ref, w_ref, cos_ref, sa_ref, sb_ref, qg_ref, kg_ref,
                   q_ref, k_ref, v_ref, f_ref, gate_ref, h_scr):
    j = pl.program_id(1)

    @pl.when(j == 0)
    def _():
        x = x_ref[...]
        ms = jnp.mean(x * x, axis=-1, keepdims=True)
        y = x * lax.rsqrt(ms + NORM_EPS) * g_ref[...]
        h_scr[...] = (y * (1.0 + mod_ref[1:2, :]) + mod_ref[0:1, :]).astype(BF16)

    h = h_scr[...]
    cos, sin_a, sin_b = cos_ref[...], sa_ref[...], sb_ref[...]
    c0 = 0
    r_q = jnp.dot(h, w_ref[:, c0:c0 + INPROJ_Q_COLS], preferred_element_type=F32)
    for hh in range(INPROJ_Q_COLS // HEAD_DIM):
        t = _head_norm_rope(r_q[:, hh * HEAD_DIM:(hh + 1) * HEAD_DIM], qg_ref[...], cos, sin_a, sin_b)
        q_ref[hh] = (t * Q_PRESCALE).astype(BF16)
    c0 += INPROJ_Q_COLS
    r_kv = jnp.dot(h, w_ref[:, c0:c0 + 2 * HEAD_DIM], preferred_element_type=F32)
    k_ref[0] = _head_norm_rope(r_kv[:, :HEAD_DIM], kg_ref[...], cos, sin_a, sin_b).astype(BF16)
    v_ref[0, 0:HEAD_DIM, :] = r_kv[:, HEAD_DIM:].T.astype(BF16)
    ones_row = lax.broadcasted_iota(jnp.int32, (VT_ROWS - HEAD_DIM, INPROJ_TM), 0) == 0
    v_ref[0, HEAD_DIM:VT_ROWS, :] = jnp.where(ones_row, 1.0, 0.0).astype(BF16)
    c0 += 2 * HEAD_DIM
    f_ref[0] = jnp.dot(h, w_ref[:, c0:c0 + FOURIER_GROUP_DIM], preferred_element_type=F32).astype(BF16)
    c0 += FOURIER_GROUP_DIM
    for s in range(INPROJ_GATE_COLS // INPROJ_GATE_SPLIT):
        r_g = jnp.dot(h, w_ref[:, c0:c0 + INPROJ_GATE_SPLIT], preferred_element_type=F32)
        gate_ref[:, s * INPROJ_GATE_SPLIT:(s + 1) * INPROJ_GATE_SPLIT] = jax.nn.sigmoid(r_g).astype(BF16)
        c0 += INPROJ_GATE_SPLIT


def _regroup_w_in(w_in):
    o1 = ATTN_Q_DIM
    o2 = o1 + ATTN_KV_DIM
    o3 = o2 + ATTN_KV_DIM
    o4 = o3 + FOURIER_DIM
    parts = []
    for j in range(INPROJ_GROUPS):
        parts += [
            w_in[:, j * INPROJ_Q_COLS:(j + 1) * INPROJ_Q_COLS],
            w_in[:, o1 + j * HEAD_DIM:o1 + (j + 1) * HEAD_DIM],
            w_in[:, o2 + j * HEAD_DIM:o2 + (j + 1) * HEAD_DIM],
            w_in[:, o3 + j * FOURIER_GROUP_DIM:o3 + (j + 1) * FOURIER_GROUP_DIM],
            w_in[:, o4 + j * INPROJ_GATE_COLS:o4 + (j + 1) * INPROJ_GATE_COLS],
        ]
    return jnp.concatenate(parts, axis=1).astype(BF16)


def _in_proj(x2, mod3, g_pre, w_grouped, cos, sin_a, sin_b, qg, kg):
    t_tokens = x2.shape[0]
    tm = INPROJ_TM
    tn = IN_PROJ_DIM // INPROJ_GROUPS
    tiles_per_seq = SEQ // tm
    row_tbl = pl.BlockSpec((tm, HEAD_DIM), lambda i, j: (i % tiles_per_seq, 0))
    vec_hd = pl.BlockSpec((1, HEAD_DIM), lambda i, j: (0, 0))
    return pl.pallas_call(
        _inproj_kernel,
        grid=(t_tokens // tm, INPROJ_GROUPS),
        in_specs=[
            pl.BlockSpec((tm, D_MODEL), lambda i, j: (i, 0)),
            pl.BlockSpec((None, 6, D_MODEL), lambda i, j: (i // tiles_per_seq, 0, 0)),
            pl.BlockSpec((1, D_MODEL), lambda i, j: (0, 0)),
            pl.BlockSpec((D_MODEL, tn), lambda i, j: (0, j)),
            row_tbl, row_tbl, row_tbl, vec_hd, vec_hd,
        ],
        out_specs=[
            pl.BlockSpec((INPROJ_Q_COLS // HEAD_DIM, tm, HEAD_DIM), lambda i, j: (j, i, 0)),
            pl.BlockSpec((1, tm, HEAD_DIM), lambda i, j: (j, i, 0)),
            pl.BlockSpec((1, VT_ROWS, tm), lambda i, j: (j, 0, i)),
            pl.BlockSpec((1, tm, FOURIER_GROUP_DIM), lambda i, j: (j, i, 0)),
            pl.BlockSpec((tm, INPROJ_GATE_COLS), lambda i, j: (i, j)),
        ],
        out_shape=[
            jax.ShapeDtypeStruct((N_HEADS, t_tokens, HEAD_DIM), BF16),
            jax.ShapeDtypeStruct((N_KV_HEADS, t_tokens, HEAD_DIM), BF16),
            jax.ShapeDtypeStruct((N_KV_HEADS, VT_ROWS, t_tokens), BF16),
            jax.ShapeDtypeStruct((N_FOURIER_GROUPS, t_tokens, FOURIER_GROUP_DIM), BF16),
            jax.ShapeDtypeStruct((t_tokens, 2 * D_MODEL), BF16),
        ],
        scratch_shapes=[pltpu.VMEM((tm, D_MODEL), BF16)],
        compiler_params=_cparams(("arbitrary", "arbitrary")),
        name="in_proj",
    )(x2, mod3, g_pre.reshape(1, D_MODEL), w_grouped, cos, sin_a, sin_b,
      qg.reshape(1, HEAD_DIM), kg.reshape(1, HEAD_DIM))


def _attn_kernel(q_ref, k_ref, vt_ref, o_ref, s_scr):
    n_chunks = SEQ // ATTN_TK
    n_sub = ATTN_TQ // ATTN_SUB

    def load_q(sub):
        return q_ref[:, sub * ATTN_SUB:(sub + 1) * ATTN_SUB, :].reshape(Q_PER_KV * ATTN_SUB, HEAD_DIM)

    def scores(q, c, slot):
        k_c = k_ref[c * ATTN_TK:(c + 1) * ATTN_TK, :]
        s_scr[slot] = lax.dot_general(k_c, q, (((1,), (1,)), ((), ())), preferred_element_type=F32)

    q = load_q(0)
    scores(q, 0, 0)
    unit = 0
    for sub in range(n_sub):
        q_next = load_q(sub + 1) if sub + 1 < n_sub else None
        m = acc = None
        for c in range(n_chunks):
            if c + 1 < n_chunks:
                scores(q, c + 1, (unit + 1) % 2)
            elif q_next is not None:
                scores(q_next, 0, (unit + 1) % 2)
            s = s_scr[unit % 2]
            vt_c = vt_ref[:, c * ATTN_TK:(c + 1) * ATTN_TK]
            m_c = jnp.max(s, axis=0, keepdims=True)
            if c == 0:
                m = m_c
                acc = jnp.dot(vt_c, jnp.exp2(s - m).astype(BF16), preferred_element_type=F32)
            else:
                m_new = jnp.maximum(m, m_c)
                alpha = jnp.exp2(m - m_new)
                acc = alpha * acc + jnp.dot(vt_c, jnp.exp2(s - m_new).astype(BF16), preferred_element_type=F32)
                m = m_new
            unit += 1
        o_t = acc[:HEAD_DIM] * (1.0 / acc[HEAD_DIM:HEAD_DIM + 1])
        r0 = sub * ATTN_SUB
        for hh in range(Q_PER_KV):
            o_ref[r0:r0 + ATTN_SUB, hh * HEAD_DIM:(hh + 1) * HEAD_DIM] = (
                o_t[:, hh * ATTN_SUB:(hh + 1) * ATTN_SUB].T.astype(BF16))
        q = q_next


def _attention(q, k, vt):
    t_tokens = q.shape[1]
    n_b = t_tokens // SEQ
    qt = SEQ // ATTN_TQ
    return pl.pallas_call(
        _attn_kernel,
        grid=(n_b, N_KV_HEADS, qt),
        in_specs=[
            pl.BlockSpec((Q_PER_KV, ATTN_TQ, HEAD_DIM), lambda b, h, i: (h, b * qt + i, 0)),
            pl.BlockSpec((None, SEQ, HEAD_DIM), lambda b, h, i: (h, b, 0)),
            pl.BlockSpec((None, VT_ROWS, SEQ), lambda b, h, i: (h, 0, b)),
        ],
        out_specs=pl.BlockSpec((ATTN_TQ, Q_PER_KV * HEAD_DIM), lambda b, h, i: (b * qt + i, h)),
        out_shape=jax.ShapeDtypeStruct((t_tokens, ATTN_Q_DIM), BF16),
        scratch_shapes=[pltpu.VMEM((2, ATTN_TK, Q_PER_KV * ATTN_SUB), F32)],
        compiler_params=_cparams(("arbitrary", "arbitrary", "arbitrary")),
        name="attention",
    )(q, k, vt)


def _fourier_tables():
    r = FFT_RADIX
    n = FOURIER_GROUP_DIM
    ch = np.arange(n)
    ang_c = 2.0 * np.pi * np.outer(ch, ch) / n
    scale = 1.0 / math.sqrt(SEQ * n)
    w_ch = np.concatenate([np.cos(ang_c), -np.sin(ang_c)], axis=1) * scale
    a = np.arange(r)
    ang1 = 2.0 * np.pi * np.outer(a, a) / r
    c1, s1 = np.cos(ang1), np.sin(ang1)
    w1 = np.zeros((r, 2, 2, r))
    w1[:, 0, 0, :] = c1
    w1[:, 0, 1, :] = s1
    w1[:, 1, 0, :] = -s1
    w1[:, 1, 1, :] = c1
    w1 = w1.reshape(2 * r, 2 * r)
    sp = (np.arange(r)[:, None] + r * np.arange(r)[None, :])
    th = 2.0 * np.pi * sp[:, :, None] * np.arange(r)[None, None, :] / SEQ
    w2 = np.concatenate([np.cos(th), np.sin(th)], axis=2)
    return tuple(jnp.asarray(t, dtype=F32).astype(BF16) for t in (w_ch, w1, w2))


def _fourier_kernel(f_ref, wch_ref, w1_ref, w2_ref, o_ref, x_scr, y_scr, o_scr):
    r = FFT_RADIX
    n = FOURIER_GROUP_DIM
    z = jnp.dot(f_ref[...], wch_ref[...], preferred_element_type=F32)
    x_scr[0:r] = z[:, :n].astype(BF16).reshape(r, r, n)
    x_scr[r:2 * r] = z[:, n:].astype(BF16).reshape(r, r, n)
    y = lax.dot_general(w1_ref[...], x_scr[...], (((1,), (0,)), ((), ())),
                        preferred_element_type=F32)
    y_scr[...] = y.astype(BF16)

    def body(c, carry):
        slab = y_scr[pl.ds(2 * c, 2)].reshape(2 * r, n)
        res = jnp.dot(w2_ref[c], slab, preferred_element_type=F32)
        for s in range(n // LANES):
            o_scr[s, pl.ds(c, r, stride=r), :] = res[:, s * LANES:(s + 1) * LANES]
        return carry

    lax.fori_loop(0, r, body, 0, unroll=FFT_STAGE2_UNROLL)
    for s in range(n // LANES):
        o_ref[:, s * LANES:(s + 1) * LANES] = o_scr[s].astype(BF16)


def _fourier(f):
    t_tokens = f.shape[1]
    n_b = t_tokens // SEQ
    r = FFT_RADIX
    n = FOURIER_GROUP_DIM
    w_ch, w1, w2 = _fourier_tables()
    blk = pl.BlockSpec((None, SEQ, n), lambda g, b: (g, b, 0))
    return pl.pallas_call(
        _fourier_kernel,
        grid=(N_FOURIER_GROUPS, n_b),
        in_specs=[
            blk,
            pl.BlockSpec((n, 2 * n), lambda g, b: (0, 0)),
            pl.BlockSpec((2 * r, 2 * r), lambda g, b: (0, 0)),
            pl.BlockSpec((r, r, 2 * r), lambda g, b: (0, 0, 0)),
        ],
        out_specs=blk,
        out_shape=jax.ShapeDtypeStruct((N_FOURIER_GROUPS, t_tokens, n), BF16),
        scratch_shapes=[
            pltpu.VMEM((2 * r, r, n), BF16),
            pltpu.VMEM((2 * r, r, n), BF16),
            pltpu.VMEM((n // LANES, SEQ, LANES), F32),
        ],
        compiler_params=_cparams(("arbitrary", "arbitrary")),
        name="fourier",
    )(f, w_ch, w1, w2)


def _post_mix_kernel(attn_ref, four_ref, gate_ref, x_ref, mod_ref, gpost_ref, gpre_ref,
                     wao_ref, wfo_ref, wmo_ref, wr_ref, br_ref,
                     x1_ref, h2_ref, logit_ref):
    a = jnp.dot(attn_ref[...], wao_ref[...], preferred_element_type=F32)
    fo = jnp.dot(four_ref[0], wfo_ref[0], preferred_element_type=F32)
    for g in range(1, N_FOURIER_GROUPS):
        fo = fo + jnp.dot(four_ref[g], wfo_ref[g], preferred_element_type=F32)
    ga = gate_ref[:, :D_MODEL].astype(F32)
    gf = gate_ref[:, D_MODEL:].astype(F32)
    m = (ga * a + gf * fo).astype(BF16)
    mixed = jnp.dot(m, wmo_ref[...], preferred_element_type=F32)
    ms = jnp.mean(mixed * mixed, axis=-1, keepdims=True)
    y = mixed * lax.rsqrt(ms + NORM_EPS) * gpost_ref[...]
    x1 = x_ref[...] + mod_ref[2:3, :] * y
    x1_ref[...] = x1
    ms1 = jnp.mean(x1 * x1, axis=-1, keepdims=True)
    h2 = (x1 * lax.rsqrt(ms1 + NORM_EPS) * gpre_ref[...]) * (1.0 + mod_ref[4:5, :]) + mod_ref[3:4, :]
    h2_ref[...] = h2
    logit_ref[...] = jnp.dot(h2.astype(BF16), wr_ref[...], preferred_element_type=F32) + br_ref[...]


def _post_mix(attn, four, gates, x2, mod3, g_post, g_pre, wao, wfo4, wmo, w_router, b_router):
    t_tokens = x2.shape[0]
    tm = POST_TM
    tiles_per_seq = SEQ // tm
    n = FOURIER_GROUP_DIM
    const2 = lambda i: (0, 0)
    resident = dict(pipeline_mode=pl.Buffered(1))
    return pl.pallas_call(
        _post_mix_kernel,
        grid=(t_tokens // tm,),
        in_specs=[
            pl.BlockSpec((tm, ATTN_Q_DIM), lambda i: (i, 0)),
            pl.BlockSpec((N_FOURIER_GROUPS, tm, n), lambda i: (0, i, 0)),
            pl.BlockSpec((tm, 2 * D_MODEL), lambda i: (i, 0)),
            pl.BlockSpec((tm, D_MODEL), lambda i: (i, 0)),
            pl.BlockSpec((None, 6, D_MODEL), lambda i: (i // tiles_per_seq, 0, 0)),
            pl.BlockSpec((1, D_MODEL), const2),
            pl.BlockSpec((1, D_MODEL), const2),
            pl.BlockSpec((ATTN_Q_DIM, D_MODEL), const2, **resident),
            pl.BlockSpec((N_FOURIER_GROUPS, n, D_MODEL), lambda i: (0, 0, 0), **resident),
            pl.BlockSpec((D_MODEL, D_MODEL), const2, **resident),
            pl.BlockSpec((D_MODEL, LANES), const2, **resident),
            pl.BlockSpec((1, LANES), const2),
        ],
        out_specs=[
            pl.BlockSpec((tm, D_MODEL), lambda i: (i, 0)),
            pl.BlockSpec((tm, D_MODEL), lambda i: (i, 0)),
            pl.BlockSpec((tm, LANES), lambda i: (i, 0)),
        ],
        out_shape=[
            jax.ShapeDtypeStruct((t_tokens, D_MODEL), F32),
            jax.ShapeDtypeStruct((t_tokens, D_MODEL), F32),
            jax.ShapeDtypeStruct((t_tokens, LANES), F32),
        ],
        compiler_params=_cparams(("arbitrary",)),
        name="post_mix",
    )(attn, four, gates, x2, mod3, g_post.reshape(1, D_MODEL), g_pre.reshape(1, D_MODEL),
      wao, wfo4, wmo, w_router, b_router)


def _first_argmax(vals, lane):
    mx = jnp.max(vals, axis=-1, keepdims=True)
    idx = jnp.min(jnp.where(vals == mx, lane, float(LANES)), axis=-1, keepdims=True)
    return mx, idx


def _route_kernel(logit_ref, tri_ref, out_ref, cnt_ref, carry_scr, start_scr):
    phase = pl.program_id(0)
    i = pl.program_id(1)

    @pl.when((phase == 0) & (i == 0))
    def _():
        carry_scr[...] = jnp.zeros_like(carry_scr)
        start_scr[...] = jnp.zeros_like(start_scr)

    @pl.when((phase == 1) & (i == 0))
    def _():
        counts = carry_scr[...]
        lane8 = lax.broadcasted_iota(jnp.int32, counts.shape, 1)
        incl = counts
        shift = 1
        while shift < N_EXPERTS:
            incl = incl + jnp.where(lane8 >= N_GROUPS + shift, pltpu.roll(incl, shift, 1), 0.0)
            shift *= 2
        start_scr[...] = incl - counts
        carry_scr[...] = jnp.zeros_like(carry_scr)

    lg = logit_ref[...]
    lane = lax.broadcasted_iota(jnp.int32, lg.shape, 1).astype(F32)
    neg = jnp.float32(-jnp.inf)
    gl = jnp.where(lane < N_GROUPS, lg, neg)
    gmax, gidx = _first_argmax(gl, lane)
    g_w = 1.0 / jnp.sum(jnp.exp(gl - gmax), axis=-1, keepdims=True)
    lo = N_GROUPS + EXPERTS_PER_GROUP * gidx
    el = jnp.where((lane >= lo) & (lane < lo + EXPERTS_PER_GROUP), lg, neg)
    m1, i1 = _first_argmax(el, lane)
    el2 = jnp.where(lane == i1, neg, el)
    m2, i2 = _first_argmax(el2, lane)
    p2 = jnp.exp(m2 - m1)
    w1 = g_w / (1.0 + p2)
    w2 = g_w * p2 / (1.0 + p2)
    oh = jnp.where((lane == i1) | (lane == i2), 1.0, 0.0)
    before = (jnp.dot(tri_ref[...], oh.astype(BF16), preferred_element_type=F32)
              + carry_scr[0:1, :] + start_scr[0:1, :])
    pos1 = jnp.sum(jnp.where(lane == i1, before, 0.0), axis=-1, keepdims=True)
    pos2 = jnp.sum(jnp.where(lane == i2, before, 0.0), axis=-1, keepdims=True)
    carry_scr[...] = carry_scr[...] + jnp.sum(oh, axis=0, keepdims=True)
    out = jnp.where(lane == 0, i1 - N_GROUPS, 0.0)
    out = jnp.where(lane == 1, i2 - N_GROUPS, out)
    out = jnp.where(lane == 2, pos1, out)
    out = jnp.where(lane == 3, pos2, out)
    out = jnp.where(lane == 4, w1, out)
    out = jnp.where(lane == 5, w2, out)
    out_ref[...] = out
    cnt_ref[...] = carry_scr[...]


def _route(logits):
    t_tokens = logits.shape[0]
    tb = ROUTE_TB
    tri = jnp.asarray(np.tril(np.ones((tb, tb), np.float32), -1), dtype=BF16)
    return pl.pallas_call(
        _route_kernel,
        grid=(2, t_tokens // tb),
        in_specs=[
            pl.BlockSpec((tb, LANES), lambda p, i: (i, 0)),
            pl.BlockSpec((tb, tb), lambda p, i: (0, 0)),
        ],
        out_specs=[
            pl.BlockSpec((tb, LANES), lambda p, i: (i * p, 0)),
            pl.BlockSpec((8, LANES), lambda p, i: (0, 0)),
        ],
        out_shape=[
            jax.ShapeDtypeStruct((t_tokens, LANES), F32),
            jax.ShapeDtypeStruct((8, LANES), F32),
        ],
        scratch_shapes=[pltpu.VMEM((8, LANES), F32), pltpu.VMEM((8, LANES), F32)],
        compiler_params=_cparams(("arbitrary", "arbitrary")),
        name="route",
    )(logits, tri)


def _row_gather(src_hbm, src_row, j, dst, sem):
    return pltpu.make_async_copy(src_hbm.at[pl.ds(src_row, 1), :], dst.at[pl.ds(j, 1), :], sem)


def _start_row_gathers(src_hbm, idx_ref, dst, sem, n_rows):
    def body(jj, carry):
        for u in range(GATHER_UNROLL):
            j = jj * GATHER_UNROLL + u
            _row_gather(src_hbm, idx_ref[0, 0, j], j, dst, sem).start(priority=u % 2)
        return carry

    lax.fori_loop(0, n_rows // GATHER_UNROLL, body, 0)


def _wait_row_gathers(src_hbm, dst, sem, n_rows):
    def body(jj, carry):
        for u in range(GATHER_UNROLL):
            _row_gather(src_hbm, 0, jj * GATHER_UNROLL + u, dst, sem).wait()
        return carry

    lax.fori_loop(0, n_rows // GATHER_UNROLL, body, 0)


def _dispatch_kernel(tok_ref, tok_next_ref, h_hbm, xs_ref, buf, sem):
    i = pl.program_id(0)
    slot = i % 2

    @pl.when(i == 0)
    def _():
        _start_row_gathers(h_hbm, tok_ref, buf.at[0], sem.at[0], MOE_TM)

    @pl.when(i + 1 < pl.num_programs(0))
    def _():
        _start_row_gathers(h_hbm, tok_next_ref, buf.at[1 - slot], sem.at[1 - slot], MOE_TM)

    _wait_row_gathers(h_hbm, buf.at[slot], sem.at[slot], MOE_TM)
    xs_ref[...] = buf[slot].astype(BF16)


def _dispatch(tok_sorted, h2):
    n_rows = tok_sorted.shape[0]
    nb = n_rows // MOE_TM
    tok3 = tok_sorted.reshape(nb, 1, MOE_TM)
    return pl.pallas_call(
        _dispatch_kernel,
        grid=(nb,),
        in_specs=[
            pl.BlockSpec((1, 1, MOE_TM), lambda i: (i, 0, 0), memory_space=pltpu.SMEM),
            pl.BlockSpec((1, 1, MOE_TM), lambda i: (jnp.minimum(i + 1, nb - 1), 0, 0), memory_space=pltpu.SMEM),
            pl.BlockSpec(memory_space=pl.ANY),
        ],
        out_specs=pl.BlockSpec((MOE_TM, D_MODEL), lambda i: (i, 0)),
        out_shape=jax.ShapeDtypeStruct((n_rows, D_MODEL), BF16),
        scratch_shapes=[pltpu.VMEM((2, MOE_TM, D_MODEL), F32), pltpu.SemaphoreType.DMA((2,))],
        compiler_params=_cparams(("arbitrary",)),
        name="dispatch",
    )(tok3, tok3, h2)


def _expert_kernel(tile_ref, exp_ref, start_ref, end_ref, nitem_ref,
                   xs_ref, wg_ref, wu_ref, wd_ref, ys_ref, wg_scr, wu_scr, wd_scr):
    k = pl.program_id(0)
    tile = tile_ref[k]
    e = exp_ref[k]
    prev = jnp.maximum(k - 1, 0)
    first_visit = (k == 0) | (tile != tile_ref[prev])
    new_expert = (k == 0) | (e != exp_ref[prev])

    @pl.when(new_expert)
    def _():
        wg_scr[...] = wg_ref[...].astype(BF16)
        wu_scr[...] = wu_ref[...].astype(BF16)
        wd_scr[...] = wd_ref[...].astype(BF16)

    @pl.when(first_visit)
    def _():
        ys_ref[...] = jnp.zeros_like(ys_ref)

    @pl.when(k < nitem_ref[0])
    def _():
        x = xs_ref[...]
        g = jnp.dot(x, wg_scr[...], preferred_element_type=F32)
        u = jnp.dot(x, wu_scr[...], preferred_element_type=F32)
        hmid = (g * jax.nn.sigmoid(g) * u).astype(BF16)
        y = jnp.dot(hmid, wd_scr[...], preferred_element_type=F32)
        row = tile * MOE_TM + lax.broadcasted_iota(jnp.int32, (MOE_TM, 1), 0)
        valid = (row >= start_ref[e]) & (row < end_ref[e])
        ys_ref[...] += jnp.where(valid, y, 0.0)


def _experts(item_tile, item_exp, starts, ends, n_items, xs, wg, wu, wd):
    n_rows = xs.shape[0]
    n_tiles = n_rows // MOE_TM
    max_items = n_tiles + N_EXPERTS - 1
    grid_spec = pltpu.PrefetchScalarGridSpec(
        num_scalar_prefetch=5,
        grid=(max_items,),
        in_specs=[
            pl.BlockSpec((MOE_TM, D_MODEL), lambda k, t, e, s, en, n: (t[k], 0)),
            pl.BlockSpec((None, D_MODEL, EXPERT_FF), lambda k, t, e, s, en, n: (e[k], 0, 0)),
            pl.BlockSpec((None, D_MODEL, EXPERT_FF), lambda k, t, e, s, en, n: (e[k], 0, 0)),
            pl.BlockSpec((None, EXPERT_FF, D_MODEL), lambda k, t, e, s, en, n: (e[k], 0, 0)),
        ],
        out_specs=pl.BlockSpec((MOE_TM, D_MODEL), lambda k, t, e, s, en, n: (t[k], 0)),
        scratch_shapes=[
            pltpu.VMEM((D_MODEL, EXPERT_FF), BF16),
            pltpu.VMEM((D_MODEL, EXPERT_FF), BF16),
            pltpu.VMEM((EXPERT_FF, D_MODEL), BF16),
        ],
    )
    return pl.pallas_call(
        _expert_kernel,
        grid_spec=grid_spec,
        out_shape=jax.ShapeDtypeStruct((n_rows, D_MODEL), F32),
        compiler_params=_cparams(("arbitrary",)),
        name="experts",
    )(item_tile, item_exp, starts, ends, n_items, xs, wg, wu, wd)


def _expert_items(counts):
    ends = jnp.cumsum(counts)
    starts = ends - counts
    first_tile = starts // MOE_TM
    last_tile = (ends - 1) // MOE_TM
    n_it = jnp.where(counts > 0, last_tile - first_tile + 1, 0)
    it_end = jnp.cumsum(n_it)
    it_start = it_end - n_it
    total = it_end[-1]
    return starts, ends, first_tile, it_start, it_end, total


def _combine_kernel(p1_ref, p2_ref, p1_next_ref, p2_next_ref, ys_hbm, route_ref, x1_ref, mod_ref, g_ref,
                    o_ref, y_scr, sem):
    i = pl.program_id(0)
    slot = i % 2

    def start_block(pa_ref, pb_ref, s):
        _start_row_gathers(ys_hbm, pa_ref, y_scr.at[s, 0], sem.at[s], MOE_TB)
        _start_row_gathers(ys_hbm, pb_ref, y_scr.at[s, 1], sem.at[s], MOE_TB)

    @pl.when(i == 0)
    def _():
        start_block(p1_ref, p2_ref, 0)

    @pl.when(i + 1 < pl.num_programs(0))
    def _():
        start_block(p1_next_ref, p2_next_ref, 1 - slot)

    _wait_row_gathers(ys_hbm, y_scr.at[slot, 0], sem.at[slot], MOE_TB)
    _wait_row_gathers(ys_hbm, y_scr.at[slot, 1], sem.at[slot], MOE_TB)
    w1 = route_ref[:, 4:5]
    w2 = route_ref[:, 5:6]
    ffn = w1 * y_scr[slot, 0] + w2 * y_scr[slot, 1]
    ms = jnp.mean(ffn * ffn, axis=-1, keepdims=True)
    y = ffn * lax.rsqrt(ms + NORM_EPS) * g_ref[...]
    o_ref[...] = x1_ref[...] + mod_ref[5:6, :] * y


def _combine(pos1, pos2, ys, route, x1, mod3, g_post):
    t_tokens = x1.shape[0]
    tb = MOE_TB
    nb = t_tokens // tb
    tiles_per_seq = SEQ // tb
    smem_blk = pl.BlockSpec((1, 1, tb), lambda i: (i, 0, 0), memory_space=pltpu.SMEM)
    smem_next = pl.BlockSpec((1, 1, tb), lambda i: (jnp.minimum(i + 1, nb - 1), 0, 0), memory_space=pltpu.SMEM)
    p1 = pos1.reshape(nb, 1, tb)
    p2 = pos2.reshape(nb, 1, tb)
    return pl.pallas_call(
        _combine_kernel,
        grid=(nb,),
        in_specs=[
            smem_blk, smem_blk, smem_next, smem_next,
            pl.BlockSpec(memory_space=pl.ANY),
            pl.BlockSpec((tb, LANES), lambda i: (i, 0)),
            pl.BlockSpec((tb, D_MODEL), lambda i: (i, 0)),
            pl.BlockSpec((None, 6, D_MODEL), lambda i: (i // tiles_per_seq, 0, 0)),
            pl.BlockSpec((1, D_MODEL), lambda i: (0, 0)),
        ],
        out_specs=pl.BlockSpec((tb, D_MODEL), lambda i: (i, 0)),
        out_shape=jax.ShapeDtypeStruct((t_tokens, D_MODEL), F32),
        scratch_shapes=[
            pltpu.VMEM((2, 2, tb, D_MODEL), F32),
            pltpu.SemaphoreType.DMA((2,)),
        ],
        compiler_params=_cparams(("arbitrary",)),
        name="combine",
    )(p1, p2, p1, p2, ys, route, x1, mod3, g_post.reshape(1, D_MODEL))


def _rope_tables():
    rows = SEQ // GRID_W
    row = jnp.repeat(jnp.arange(rows, dtype=F32), GRID_W)
    col = jnp.tile(jnp.arange(GRID_W, dtype=F32), rows)
    n_freq = HEAD_DIM // 4
    inv = ROPE_THETA ** (-jnp.arange(n_freq, dtype=F32) / n_freq)
    ang_r = row[:, None] * inv
    ang_c = col[:, None] * inv
    ang = jnp.concatenate([ang_r, ang_r, ang_c, ang_c], axis=-1)
    cos, sin = jnp.cos(ang), jnp.sin(ang)
    first = (jnp.arange(HEAD_DIM) % (2 * n_freq)) < n_freq
    sin_a = jnp.where(first, -sin, 0.0)
    sin_b = jnp.where(first, 0.0, sin)
    return cos, sin_a, sin_b


def kernel(x, c, w_ada, b_ada, g_pre_mix, g_post_mix, w_in, q_norm_g, k_norm_g, w_attn_out, w_fourier_out,
           w_mix_out, g_pre_ffn, g_post_ffn, w_group_router, b_group_router, w_expert_router, b_expert_router,
           w_exp_gate, w_exp_up, w_exp_down):
    n_b, seq, d = x.shape
    assert seq == SEQ and d == D_MODEL
    t_tokens = n_b * seq
    x2 = x.reshape(t_tokens, d)

    c_pad = jnp.zeros((8, d), F32).at[:n_b].set(c)
    mod = _ada(c_pad, w_ada, b_ada)[:n_b]
    mod3 = mod.reshape(n_b, 6, d)

    cos, sin_a, sin_b = _rope_tables()
    q, k, vt, f, gates = _in_proj(x2, mod3, g_pre_mix, _regroup_w_in(w_in), cos, sin_a, sin_b, q_norm_g, k_norm_g)
    attn = _attention(q, k, vt)
    four = _fourier(f)

    n_r = N_GROUPS + N_EXPERTS
    w_router = jnp.zeros((d, LANES), F32).at[:, :N_GROUPS].set(w_group_router).at[:, N_GROUPS:n_r].set(w_expert_router)
    b_router = jnp.zeros((1, LANES), F32).at[0, :N_GROUPS].set(b_group_router).at[0, N_GROUPS:n_r].set(b_expert_router)
    x1, h2, logits = _post_mix(
        attn, four, gates, x2, mod3, g_post_mix, g_pre_ffn,
        w_attn_out.astype(BF16),
        w_fourier_out.astype(BF16).reshape(N_FOURIER_GROUPS, FOURIER_GROUP_DIM, d),
        w_mix_out.astype(BF16), w_router.astype(BF16), b_router)

    route, cnt = _route(logits)
    counts = cnt[0, N_GROUPS:n_r].astype(jnp.int32)
    starts, ends, first_tile, it_start, it_end, total = _expert_items(counts)
    pos1 = route[:, 2].astype(jnp.int32)
    pos2 = route[:, 3].astype(jnp.int32)

    tok = jnp.arange(t_tokens, dtype=jnp.int32)
    tok_sorted = (jnp.zeros((2 * t_tokens,), jnp.int32)
                  .at[pos1].set(tok, unique_indices=True)
                  .at[pos2].set(tok, unique_indices=True))
    xs = _dispatch(tok_sorted, h2)

    n_tiles = 2 * t_tokens // MOE_TM
    max_items = n_tiles + N_EXPERTS - 1
    kk = jnp.minimum(jnp.arange(max_items, dtype=jnp.int32), total - 1)
    item_exp = jnp.sum((it_end[None, :] <= kk[:, None]).astype(jnp.int32), axis=1)
    item_tile = first_tile[item_exp] + kk - it_start[item_exp]
    ys = _experts(item_tile.astype(jnp.int32), item_exp.astype(jnp.int32), starts.astype(jnp.int32),
                  ends.astype(jnp.int32), total.reshape(1).astype(jnp.int32), xs, w_exp_gate, w_exp_up, w_exp_down)

    out = _combine(pos1, pos2, ys, route, x1, mod3, g_post_ffn)
    return out.reshape(n_b, seq, d)
```

```python
import math

import numpy as np
import jax
import jax.numpy as jnp
from jax import lax
from jax.experimental import pallas as pl
from jax.experimental.pallas import tpu as pltpu

F32 = jnp.float32
BF16 = jnp.bfloat16

D_MODEL = 2048
SEQ = 4096
N_HEADS = 16
N_KV_HEADS = 4
Q_PER_KV = N_HEADS // N_KV_HEADS
HEAD_DIM = 128
ROPE_THETA = 10000.0
GRID_W = 64
ATTN_Q_DIM = N_HEADS * HEAD_DIM
ATTN_KV_DIM = N_KV_HEADS * HEAD_DIM
N_FOURIER_GROUPS = 4
FOURIER_GROUP_DIM = 256
FOURIER_DIM = N_FOURIER_GROUPS * FOURIER_GROUP_DIM
IN_PROJ_DIM = ATTN_Q_DIM + 2 * ATTN_KV_DIM + FOURIER_DIM + 2 * D_MODEL
N_GROUPS = 4
EXPERTS_PER_GROUP = 8
N_EXPERTS = N_GROUPS * EXPERTS_PER_GROUP
EXPERT_FF = 512
NORM_EPS = 1e-6

LANES = 128
FFT_RADIX = 64
FFT_STAGE2_UNROLL = 8
VMEM_LIMIT = 56 * 1024 * 1024

Q_PRESCALE = HEAD_DIM ** -0.5 * math.log2(math.e)

ADA_TN = 1024
INPROJ_TM = 512
INPROJ_GROUPS = N_KV_HEADS
INPROJ_Q_COLS = ATTN_Q_DIM // INPROJ_GROUPS
INPROJ_GATE_COLS = 2 * D_MODEL // INPROJ_GROUPS
INPROJ_GATE_SPLIT = 512
VT_ROWS = HEAD_DIM + 16
ATTN_TQ = 512
ATTN_SUB = 128
ATTN_TK = 512
POST_TM = 256
ROUTE_TB = 512
MOE_TB = 256
MOE_TM = 256
GATHER_UNROLL = 8


def _cparams(sem, vmem=VMEM_LIMIT):
    return pltpu.CompilerParams(dimension_semantics=sem, vmem_limit_bytes=vmem)


def _ada_kernel(c_ref, w_ref, b_ref, o_ref):
    cs = c_ref[...]
    s = cs * jax.nn.sigmoid(cs)
    o_ref[...] = jnp.dot(s.astype(BF16), w_ref[...].astype(BF16), preferred_element_type=F32) + b_ref[...]


def _ada(c_pad, w_ada, b_ada):
    n = w_ada.shape[1]
    return pl.pallas_call(
        _ada_kernel,
        grid=(n // ADA_TN,),
        in_specs=[
            pl.BlockSpec((8, D_MODEL), lambda j: (0, 0)),
            pl.BlockSpec((D_MODEL, ADA_TN), lambda j: (0, j)),
            pl.BlockSpec((1, ADA_TN), lambda j: (0, j)),
        ],
        out_specs=pl.BlockSpec((8, ADA_TN), lambda j: (0, j)),
        out_shape=jax.ShapeDtypeStruct((8, n), F32),
        compiler_params=_cparams(("arbitrary",)),
        name="ada",
    )(c_pad, w_ada, b_ada.reshape(1, n))


def _head_norm_rope(t, g, cos, sin_a, sin_b):
    ms = jnp.mean(t * t, axis=-1, keepdims=True)
    y = t * lax.rsqrt(ms + NORM_EPS) * g
    return y * cos + pltpu.roll(y, 96, 1) * sin_a + pltpu.roll(y, 32, 1) * sin_b


def _inproj_kernel(x_ref, mod_ref, g_ref, w_ref, cos_ref, sa_ref, sb_ref, qg_ref, kg_ref,
                   q_ref, k_ref, v_ref, f_ref, gate_ref, h_scr):
    j = pl.program_id(1)

    @pl.when(j == 0)
    def _():
        x = x_ref[...]
        ms = jnp.mean(x * x, axis=-1, keepdims=True)
        y = x * lax.rsqrt(ms + NORM_EPS) * g_ref[...]
        h_scr[...] = (y * (1.0 + mod_ref[1:2, :]) + mod_ref[0:1, :]).astype(BF16)

    h = h_scr[...]
    cos, sin_a, sin_b = cos_ref[...], sa_ref[...], sb_ref[...]
    c0 = 0
    r_q = jnp.dot(h, w_ref[:, c0:c0 + INPROJ_Q_COLS], preferred_element_type=F32)
    for hh in range(INPROJ_Q_COLS // HEAD_DIM):
        t = _head_norm_rope(r_q[:, hh * HEAD_DIM:(hh + 1) * HEAD_DIM], qg_ref[...], cos, sin_a, sin_b)
        q_ref[hh] = (t * Q_PRESCALE).astype(BF16)
    c0 += INPROJ_Q_COLS
    r_kv = jnp.dot(h, w_ref[:, c0:c0 + 2 * HEAD_DIM], preferred_element_type=F32)
    k_ref[0] = _head_norm_rope(r_kv[:, :HEAD_DIM], kg_ref[...], cos, sin_a, sin_b).astype(BF16)
    v_ref[0, 0:HEAD_DIM, :] = r_kv[:, HEAD_DIM:].T.astype(BF16)
    ones_row = lax.broadcasted_iota(jnp.int32, (VT_ROWS - HEAD_DIM, INPROJ_TM), 0) == 0
    v_ref[0, HEAD_DIM:VT_ROWS, :] = jnp.where(ones_row, 1.0, 0.0).astype(BF16)
    c0 += 2 * HEAD_DIM
    f_ref[0] = jnp.dot(h, w_ref[:, c0:c0 + FOURIER_GROUP_DIM], preferred_element_type=F32).astype(BF16)
    c0 += FOURIER_GROUP_DIM
    for s in range(INPROJ_GATE_COLS // INPROJ_GATE_SPLIT):
        r_g = jnp.dot(h, w_ref[:, c0:c0 + INPROJ_GATE_SPLIT], preferred_element_type=F32)
        gate_ref[:, s * INPROJ_GATE_SPLIT:(s + 1) * INPROJ_GATE_SPLIT] = jax.nn.sigmoid(r_g).astype(BF16)
        c0 += INPROJ_GATE_SPLIT


def _regroup_w_in(w_in):
    o1 = ATTN_Q_DIM
    o2 = o1 + ATTN_KV_DIM
    o3 = o2 + ATTN_KV_DIM
    o4 = o3 + FOURIER_DIM
    parts = []
    for j in range(INPROJ_GROUPS):
        parts += [
            w_in[:, j * INPROJ_Q_COLS:(j + 1) * INPROJ_Q_COLS],
            w_in[:, o1 + j * HEAD_DIM:o1 + (j + 1) * HEAD_DIM],
            w_in[:, o2 + j * HEAD_DIM:o2 + (j + 1) * HEAD_DIM],
            w_in[:, o3 + j * FOURIER_GROUP_DIM:o3 + (j + 1) * FOURIER_GROUP_DIM],
            w_in[:, o4 + j * INPROJ_GATE_COLS:o4 + (j + 1) * INPROJ_GATE_COLS],
        ]
    return jnp.concatenate(parts, axis=1).astype(BF16)


def _in_proj(x2, mod3, g_pre, w_grouped, cos, sin_a, sin_b, qg, kg):
    t_tokens = x2.shape[0]
    tm = INPROJ_TM
    tn = IN_PROJ_DIM // INPROJ_GROUPS
    tiles_per_seq = SEQ // tm
    row_tbl = pl.BlockSpec((tm, HEAD_DIM), lambda i, j: (i % tiles_per_seq, 0))
    vec_hd = pl.BlockSpec((1, HEAD_DIM), lambda i, j: (0, 0))
    return pl.pallas_call(
        _inproj_kernel,
        grid=(t_tokens // tm, INPROJ_GROUPS),
        in_specs=[
            pl.BlockSpec((tm, D_MODEL), lambda i, j: (i, 0)),
            pl.BlockSpec((None, 6, D_MODEL), lambda i, j: (i // tiles_per_seq, 0, 0)),
            pl.BlockSpec((1, D_MODEL), lambda i, j: (0, 0)),
            pl.BlockSpec((D_MODEL, tn), lambda i, j: (0, j)),
            row_tbl, row_tbl, row_tbl, vec_hd, vec_hd,
        ],
        out_specs=[
            pl.BlockSpec((INPROJ_Q_COLS // HEAD_DIM, tm, HEAD_DIM), lambda i, j: (j, i, 0)),
            pl.BlockSpec((1, tm, HEAD_DIM), lambda i, j: (j, i, 0)),
            pl.BlockSpec((1, VT_ROWS, tm), lambda i, j: (j, 0, i)),
            pl.BlockSpec((1, tm, FOURIER_GROUP_DIM), lambda i, j: (j, i, 0)),
            pl.BlockSpec((tm, INPROJ_GATE_COLS), lambda i, j: (i, j)),
        ],
        out_shape=[
            jax.ShapeDtypeStruct((N_HEADS, t_tokens, HEAD_DIM), BF16),
            jax.ShapeDtypeStruct((N_KV_HEADS, t_tokens, HEAD_DIM), BF16),
            jax.ShapeDtypeStruct((N_KV_HEADS, VT_ROWS, t_tokens), BF16),
            jax.ShapeDtypeStruct((N_FOURIER_GROUPS, t_tokens, FOURIER_GROUP_DIM), BF16),
            jax.ShapeDtypeStruct((t_tokens, 2 * D_MODEL), BF16),
        ],
        scratch_shapes=[pltpu.VMEM((tm, D_MODEL), BF16)],
        compiler_params=_cparams(("arbitrary", "arbitrary")),
        name="in_proj",
    )(x2, mod3, g_pre.reshape(1, D_MODEL), w_grouped, cos, sin_a, sin_b,
      qg.reshape(1, HEAD_DIM), kg.reshape(1, HEAD_DIM))


def _attn_kernel(q_ref, k_ref, vt_ref, o_ref, s_scr):
    n_chunks = SEQ // ATTN_TK
    n_sub = ATTN_TQ // ATTN_SUB

    def load_q(sub):
        return q_ref[:, sub * ATTN_SUB:(sub + 1) * ATTN_SUB, :].reshape(Q_PER_KV * ATTN_SUB, HEAD_DIM)

    def scores(q, c, slot):
        k_c = k_ref[c * ATTN_TK:(c + 1) * ATTN_TK, :]
        s_scr[slot] = lax.dot_general(k_c, q, (((1,), (1,)), ((), ())), preferred_element_type=F32)

    qs = [load_q(sub) for sub in range(n_sub)]
    for sub in range(n_sub):
        scores(qs[sub], 0, 2 * sub)
    m = [None] * n_sub
    acc = [None] * n_sub
    for c in range(n_chunks):
        vt_c = vt_ref[:, c * ATTN_TK:(c + 1) * ATTN_TK]
        for sub in range(n_sub):
            if c + 1 < n_chunks:
                scores(qs[sub], c + 1, 2 * sub + (c + 1) % 2)
            s = s_scr[2 * sub + c % 2]
            m_c = jnp.max(s, axis=0, keepdims=True)
            if c == 0:
                m[sub] = m_c
                acc[sub] = jnp.dot(vt_c, jnp.exp2(s - m_c).astype(BF16), preferred_element_type=F32)
            else:
                m_new = jnp.maximum(m[sub], m_c)
                alpha = jnp.exp2(m[sub] - m_new)
                acc[sub] = alpha * acc[sub] + jnp.dot(vt_c, jnp.exp2(s - m_new).astype(BF16),
                                                      preferred_element_type=F32)
                m[sub] = m_new
    for sub in range(n_sub):
        o_t = acc[sub][:HEAD_DIM] * (1.0 / acc[sub][HEAD_DIM:HEAD_DIM + 1])
        r0 = sub * ATTN_SUB
        for hh in range(Q_PER_KV):
            o_ref[r0:r0 + ATTN_SUB, hh * HEAD_DIM:(hh + 1) * HEAD_DIM] = (
                o_t[:, hh * ATTN_SUB:(hh + 1) * ATTN_SUB].T.astype(BF16))


def _attention(q, k, vt):
    t_tokens = q.shape[1]
    n_b = t_tokens // SEQ
    qt = SEQ // ATTN_TQ
    return pl.pallas_call(
        _attn_kernel,
        grid=(n_b, N_KV_HEADS, qt),
        in_specs=[
            pl.BlockSpec((Q_PER_KV, ATTN_TQ, HEAD_DIM), lambda b, h, i: (h, b * qt + i, 0)),
            pl.BlockSpec((None, SEQ, HEAD_DIM), lambda b, h, i: (h, b, 0)),
            pl.BlockSpec((None, VT_ROWS, SEQ), lambda b, h, i: (h, 0, b)),
        ],
        out_specs=pl.BlockSpec((ATTN_TQ, Q_PER_KV * HEAD_DIM), lambda b, h, i: (b * qt + i, h)),
        out_shape=jax.ShapeDtypeStruct((t_tokens, ATTN_Q_DIM), BF16),
        scratch_shapes=[pltpu.VMEM((2 * (ATTN_TQ // ATTN_SUB), ATTN_TK, Q_PER_KV * ATTN_SUB), F32)],
        compiler_params=_cparams(("arbitrary", "arbitrary", "arbitrary")),
        name="attention",
    )(q, k, vt)


def _fourier_tables():
    r = FFT_RADIX
    n = FOURIER_GROUP_DIM
    ch = np.arange(n)
    ang_c = 2.0 * np.pi * np.outer(ch, ch) / n
    scale = 1.0 / math.sqrt(SEQ * n)
    w_ch = np.concatenate([np.cos(ang_c), -np.sin(ang_c)], axis=1) * scale
    a = np.arange(r)
    ang1 = 2.0 * np.pi * np.outer(a, a) / r
    c1, s1 = np.cos(ang1), np.sin(ang1)
    w1 = np.zeros((r, 2, 2, r))
    w1[:, 0, 0, :] = c1
    w1[:, 0, 1, :] = s1
    w1[:, 1, 0, :] = -s1
    w1[:, 1, 1, :] = c1
    w1 = w1.reshape(2 * r, 2 * r)
    sp = (np.arange(r)[:, None] + r * np.arange(r)[None, :])
    th = 2.0 * np.pi * sp[:, :, None] * np.arange(r)[None, None, :] / SEQ
    w2 = np.concatenate([np.cos(th), np.sin(th)], axis=2)
    return tuple(jnp.asarray(t, dtype=F32).astype(BF16) for t in (w_ch, w1, w2))


def _fourier_kernel(f_ref, wch_ref, w1_ref, w2_ref, o_ref, x_scr, y_scr, o_scr):
    r = FFT_RADIX
    n = FOURIER_GROUP_DIM
    z = jnp.dot(f_ref[...], wch_ref[...], preferred_element_type=F32)
    x_scr[0:r] = z[:, :n].astype(BF16).reshape(r, r, n)
    x_scr[r:2 * r] = z[:, n:].astype(BF16).reshape(r, r, n)
    y = lax.dot_general(w1_ref[...], x_scr[...], (((1,), (0,)), ((), ())),
                        preferred_element_type=F32)
    y_scr[...] = y.astype(BF16)

    def body(c, carry):
        slab = y_scr[pl.ds(2 * c, 2)].reshape(2 * r, n)
        res = jnp.dot(w2_ref[c], slab, preferred_element_type=F32)
        for s in range(n // LANES):
            o_scr[s, pl.ds(c, r, stride=r), :] = res[:, s * LANES:(s + 1) * LANES]
        return carry

    lax.fori_loop(0, r, body, 0, unroll=FFT_STAGE2_UNROLL)
    for s in range(n // LANES):
        o_ref[:, s * LANES:(s + 1) * LANES] = o_scr[s].astype(BF16)


def _fourier(f):
    t_tokens = f.shape[1]
    n_b = t_tokens // SEQ
    r = FFT_RADIX
    n = FOURIER_GROUP_DIM
    w_ch, w1, w2 = _fourier_tables()
    blk = pl.BlockSpec((None, SEQ, n), lambda g, b: (g, b, 0))
    return pl.pallas_call(
        _fourier_kernel,
        grid=(N_FOURIER_GROUPS, n_b),
        in_specs=[
            blk,
            pl.BlockSpec((n, 2 * n), lambda g, b: (0, 0)),
            pl.BlockSpec((2 * r, 2 * r), lambda g, b: (0, 0)),
            pl.BlockSpec((r, r, 2 * r), lambda g, b: (0, 0, 0)),
        ],
        out_specs=blk,
        out_shape=jax.ShapeDtypeStruct((N_FOURIER_GROUPS, t_tokens, n), BF16),
        scratch_shapes=[
            pltpu.VMEM((2 * r, r, n), BF16),
            pltpu.VMEM((2 * r, r, n), BF16),
            pltpu.VMEM((n // LANES, SEQ, LANES), F32),
        ],
        compiler_params=_cparams(("arbitrary", "arbitrary")),
        name="fourier",
    )(f, w_ch, w1, w2)


def _post_mix_kernel(attn_ref, four_ref, gate_ref, x_ref, mod_ref, gpost_ref, gpre_ref,
                     wao_ref, wfo_ref, wmo_ref, wr_ref, br_ref,
                     x1_ref, h2_ref, logit_ref):
    a = jnp.dot(attn_ref[...], wao_ref[...], preferred_element_type=F32)
    fo = jnp.dot(four_ref[0], wfo_ref[0], preferred_element_type=F32)
    for g in range(1, N_FOURIER_GROUPS):
        fo = fo + jnp.dot(four_ref[g], wfo_ref[g], preferred_element_type=F32)
    ga = gate_ref[:, :D_MODEL].astype(F32)
    gf = gate_ref[:, D_MODEL:].astype(F32)
    m = (ga * a + gf * fo).astype(BF16)
    mixed = jnp.dot(m, wmo_ref[...], preferred_element_type=F32)
    ms = jnp.mean(mixed * mixed, axis=-1, keepdims=True)
    y = mixed * lax.rsqrt(ms + NORM_EPS) * gpost_ref[...]
    x1 = x_ref[...] + mod_ref[2:3, :] * y
    x1_ref[...] = x1
    ms1 = jnp.mean(x1 * x1, axis=-1, keepdims=True)
    h2 = (x1 * lax.rsqrt(ms1 + NORM_EPS) * gpre_ref[...]) * (1.0 + mod_ref[4:5, :]) + mod_ref[3:4, :]
    h2_ref[...] = h2
    logit_ref[...] = jnp.dot(h2.astype(BF16), wr_ref[...], preferred_element_type=F32) + br_ref[...]


def _post_mix(attn, four, gates, x2, mod3, g_post, g_pre, wao, wfo4, wmo, w_router, b_router):
    t_tokens = x2.shape[0]
    tm = POST_TM
    tiles_per_seq = SEQ // tm
    n = FOURIER_GROUP_DIM
    const2 = lambda i: (0, 0)
    resident = dict(pipeline_mode=pl.Buffered(1))
    return pl.pallas_call(
        _post_mix_kernel,
        grid=(t_tokens // tm,),
        in_specs=[
            pl.BlockSpec((tm, ATTN_Q_DIM), lambda i: (i, 0)),
            pl.BlockSpec((N_FOURIER_GROUPS, tm, n), lambda i: (0, i, 0)),
            pl.BlockSpec((tm, 2 * D_MODEL), lambda i: (i, 0)),
            pl.BlockSpec((tm, D_MODEL), lambda i: (i, 0)),
            pl.BlockSpec((None, 6, D_MODEL), lambda i: (i // tiles_per_seq, 0, 0)),
            pl.BlockSpec((1, D_MODEL), const2),
            pl.BlockSpec((1, D_MODEL), const2),
            pl.BlockSpec((ATTN_Q_DIM, D_MODEL), const2, **resident),
            pl.BlockSpec((N_FOURIER_GROUPS, n, D_MODEL), lambda i: (0, 0, 0), **resident),
            pl.BlockSpec((D_MODEL, D_MODEL), const2, **resident),
            pl.BlockSpec((D_MODEL, LANES), const2, **resident),
            pl.BlockSpec((1, LANES), const2),
        ],
        out_specs=[
            pl.BlockSpec((tm, D_MODEL), lambda i: (i, 0)),
            pl.BlockSpec((tm, D_MODEL), lambda i: (i, 0)),
            pl.BlockSpec((tm, LANES), lambda i: (i, 0)),
        ],
        out_shape=[
            jax.ShapeDtypeStruct((t_tokens, D_MODEL), F32),
            jax.ShapeDtypeStruct((t_tokens, D_MODEL), F32),
            jax.ShapeDtypeStruct((t_tokens, LANES), F32),
        ],
        compiler_params=_cparams(("arbitrary",)),
        name="post_mix",
    )(attn, four, gates, x2, mod3, g_post.reshape(1, D_MODEL), g_pre.reshape(1, D_MODEL),
      wao, wfo4, wmo, w_router, b_router)


def _first_argmax(vals, lane):
    mx = jnp.max(vals, axis=-1, keepdims=True)
    idx = jnp.min(jnp.where(vals == mx, lane, float(LANES)), axis=-1, keepdims=True)
    return mx, idx


def _route_kernel(logit_ref, tri_ref, out_ref, cnt_ref, carry_scr, start_scr):
    phase = pl.program_id(0)
    i = pl.program_id(1)

    @pl.when((phase == 0) & (i == 0))
    def _():
        carry_scr[...] = jnp.zeros_like(carry_scr)
        start_scr[...] = jnp.zeros_like(start_scr)

    @pl.when((phase == 1) & (i == 0))
    def _():
        counts = carry_scr[...]
        lane8 = lax.broadcasted_iota(jnp.int32, counts.shape, 1)
        incl = counts
        shift = 1
        while shift < N_EXPERTS:
            incl = incl + jnp.where(lane8 >= N_GROUPS + shift, pltpu.roll(incl, shift, 1), 0.0)
            shift *= 2
        start_scr[...] = incl - counts
        carry_scr[...] = jnp.zeros_like(carry_scr)

    lg = logit_ref[...]
    lane = lax.broadcasted_iota(jnp.int32, lg.shape, 1).astype(F32)
    neg = jnp.float32(-jnp.inf)
    gl = jnp.where(lane < N_GROUPS, lg, neg)
    gmax, gidx = _first_argmax(gl, lane)
    g_w = 1.0 / jnp.sum(jnp.exp(gl - gmax), axis=-1, keepdims=True)
    lo = N_GROUPS + EXPERTS_PER_GROUP * gidx
    el = jnp.where((lane >= lo) & (lane < lo + EXPERTS_PER_GROUP), lg, neg)
    m1, i1 = _first_argmax(el, lane)
    el2 = jnp.where(lane == i1, neg, el)
    m2, i2 = _first_argmax(el2, lane)
    p2 = jnp.exp(m2 - m1)
    w1 = g_w / (1.0 + p2)
    w2 = g_w * p2 / (1.0 + p2)
    oh = jnp.where((lane == i1) | (lane == i2), 1.0, 0.0)
    before = (jnp.dot(tri_ref[...], oh.astype(BF16), preferred_element_type=F32)
              + carry_scr[0:1, :] + start_scr[0:1, :])
    pos1 = jnp.sum(jnp.where(lane == i1, before, 0.0), axis=-1, keepdims=True)
    pos2 = jnp.sum(jnp.where(lane == i2, before, 0.0), axis=-1, keepdims=True)
    carry_scr[...] = carry_scr[...] + jnp.sum(oh, axis=0, keepdims=True)
    out = jnp.where(lane == 0, i1 - N_GROUPS, 0.0)
    out = jnp.where(lane == 1, i2 - N_GROUPS, out)
    out = jnp.where(lane == 2, pos1, out)
    out = jnp.where(lane == 3, pos2, out)
    out = jnp.where(lane == 4, w1, out)
    out = jnp.where(lane == 5, w2, out)
    out_ref[...] = out
    cnt_ref[...] = carry_scr[...]


def _route(logits):
    t_tokens = logits.shape[0]
    tb = ROUTE_TB
    tri = jnp.asarray(np.tril(np.ones((tb, tb), np.float32), -1), dtype=BF16)
    return pl.pallas_call(
        _route_kernel,
        grid=(2, t_tokens // tb),
        in_specs=[
            pl.BlockSpec((tb, LANES), lambda p, i: (i, 0)),
            pl.BlockSpec((tb, tb), lambda p, i: (0, 0)),
        ],
        out_specs=[
            pl.BlockSpec((tb, LANES), lambda p, i: (i * p, 0)),
            pl.BlockSpec((8, LANES), lambda p, i: (0, 0)),
        ],
        out_shape=[
            jax.ShapeDtypeStruct((t_tokens, LANES), F32),
            jax.ShapeDtypeStruct((8, LANES), F32),
        ],
        scratch_shapes=[pltpu.VMEM((8, LANES), F32), pltpu.VMEM((8, LANES), F32)],
        compiler_params=_cparams(("arbitrary", "arbitrary")),
        name="route",
    )(logits, tri)


def _row_gather(src_hbm, src_row, j, dst, sem):
    return pltpu.make_async_copy(src_hbm.at[pl.ds(src_row, 1), :], dst.at[pl.ds(j, 1), :], sem)


def _start_row_gathers(src_hbm, idx_ref, dst, sem, n_rows):
    def body(jj, carry):
        for u in range(GATHER_UNROLL):
            j = jj * GATHER_UNROLL + u
            _row_gather(src_hbm, idx_ref[0, 0, j], j, dst, sem).start(priority=u % 2)
        return carry

    lax.fori_loop(0, n_rows // GATHER_UNROLL, body, 0)


def _wait_row_gathers(src_hbm, dst, sem, n_rows):
    def body(jj, carry):
        for u in range(GATHER_UNROLL):
            _row_gather(src_hbm, 0, jj * GATHER_UNROLL + u, dst, sem).wait()
        return carry

    lax.fori_loop(0, n_rows // GATHER_UNROLL, body, 0)


def _row_scatter(src, j, dst_hbm, dst_row, sem):
    return pltpu.make_async_copy(src.at[pl.ds(j, 1), :], dst_hbm.at[pl.ds(dst_row, 1), :], sem)


def _dispatch_kernel(p1_ref, p2_ref, h_ref, xs_hbm, buf, sem):
    i = pl.program_id(0)
    slot = i % 2
    src = buf.at[slot]
    src[...] = h_ref[...]

    def start(jj, carry):
        for u in range(GATHER_UNROLL):
            j = jj * GATHER_UNROLL + u
            _row_scatter(src, j, xs_hbm, p1_ref[0, 0, j], sem.at[slot]).start(priority=0)
            _row_scatter(src, j, xs_hbm, p2_ref[0, 0, j], sem.at[slot]).start(priority=1)
        return carry

    lax.fori_loop(0, MOE_TB // GATHER_UNROLL, start, 0)

    def wait_block(s):
        def body(jj, carry):
            for u in range(GATHER_UNROLL):
                j = jj * GATHER_UNROLL + u
                _row_scatter(buf.at[s], j, xs_hbm, 0, sem.at[s]).wait()
                _row_scatter(buf.at[s], j, xs_hbm, 0, sem.at[s]).wait()
            return carry

        lax.fori_loop(0, MOE_TB // GATHER_UNROLL, body, 0)

    @pl.when(i > 0)
    def _():
        wait_block(1 - slot)

    @pl.when(i + 1 == pl.num_programs(0))
    def _():
        wait_block(slot)


def _dispatch(pos1, pos2, h2):
    t_tokens = h2.shape[0]
    nb = t_tokens // MOE_TB
    smem_blk = pl.BlockSpec((1, 1, MOE_TB), lambda i: (i, 0, 0), memory_space=pltpu.SMEM)
    return pl.pallas_call(
        _dispatch_kernel,
        grid=(nb,),
        in_specs=[smem_blk, smem_blk, pl.BlockSpec((MOE_TB, D_MODEL), lambda i: (i, 0))],
        out_specs=pl.BlockSpec(memory_space=pl.ANY),
        out_shape=jax.ShapeDtypeStruct((2 * t_tokens, D_MODEL), F32),
        scratch_shapes=[pltpu.VMEM((2, MOE_TB, D_MODEL), F32), pltpu.SemaphoreType.DMA((2,))],
        compiler_params=_cparams(("arbitrary",)),
        name="dispatch",
    )(pos1.reshape(nb, 1, MOE_TB), pos2.reshape(nb, 1, MOE_TB), h2)


def _expert_kernel(tile_ref, exp_ref, start_ref, end_ref, nitem_ref,
                   xs_ref, wg_ref, wu_ref, wd_ref, ys_ref, wg_scr, wu_scr, wd_scr):
    k = pl.program_id(0)
    tile = tile_ref[k]
    e = exp_ref[k]
    prev = jnp.maximum(k - 1, 0)
    first_visit = (k == 0) | (tile != tile_ref[prev])
    new_expert = (k == 0) | (e != exp_ref[prev])

    @pl.when(new_expert)
    def _():
        wg_scr[...] = wg_ref[...].astype(BF16)
        wu_scr[...] = wu_ref[...].astype(BF16)
        wd_scr[...] = wd_ref[...].astype(BF16)

    @pl.when(first_visit)
    def _():
        ys_ref[...] = jnp.zeros_like(ys_ref)

    @pl.when(k < nitem_ref[0])
    def _():
        x = xs_ref[...].astype(BF16)
        g = jnp.dot(x, wg_scr[...], preferred_element_type=F32)
        u = jnp.dot(x, wu_scr[...], preferred_element_type=F32)
        hmid = (g * jax.nn.sigmoid(g) * u).astype(BF16)
        y = jnp.dot(hmid, wd_scr[...], preferred_element_type=F32)
        row = tile * MOE_TM + lax.broadcasted_iota(jnp.int32, (MOE_TM, 1), 0)
        valid = (row >= start_ref[e]) & (row < end_ref[e])
        ys_ref[...] += jnp.where(valid, y, 0.0)


def _experts(item_tile, item_exp, starts, ends, n_items, xs, wg, wu, wd):
    n_rows = xs.shape[0]
    n_tiles = n_rows // MOE_TM
    max_items = n_tiles + N_EXPERTS - 1
    grid_spec = pltpu.PrefetchScalarGridSpec(
        num_scalar_prefetch=5,
        grid=(max_items,),
        in_specs=[
            pl.BlockSpec((MOE_TM, D_MODEL), lambda k, t, e, s, en, n: (t[k], 0)),
            pl.BlockSpec((None, D_MODEL, EXPERT_FF), lambda k, t, e, s, en, n: (e[k], 0, 0)),
            pl.BlockSpec((None, D_MODEL, EXPERT_FF), lambda k, t, e, s, en, n: (e[k], 0, 0)),
            pl.BlockSpec((None, EXPERT_FF, D_MODEL), lambda k, t, e, s, en, n: (e[k], 0, 0)),
        ],
        out_specs=pl.BlockSpec((MOE_TM, D_MODEL), lambda k, t, e, s, en, n: (t[k], 0)),
        scratch_shapes=[
            pltpu.VMEM((D_MODEL, EXPERT_FF), BF16),
            pltpu.VMEM((D_MODEL, EXPERT_FF), BF16),
            pltpu.VMEM((EXPERT_FF, D_MODEL), BF16),
        ],
    )
    return pl.pallas_call(
        _expert_kernel,
        grid_spec=grid_spec,
        out_shape=jax.ShapeDtypeStruct((n_rows, D_MODEL), F32),
        compiler_params=_cparams(("arbitrary",)),
        name="experts",
    )(item_tile, item_exp, starts, ends, n_items, xs, wg, wu, wd)


def _expert_items(counts):
    ends = jnp.cumsum(counts)
    starts = ends - counts
    first_tile = starts // MOE_TM
    last_tile = (ends - 1) // MOE_TM
    n_it = jnp.where(counts > 0, last_tile - first_tile + 1, 0)
    it_end = jnp.cumsum(n_it)
    it_start = it_end - n_it
    total = it_end[-1]
    return starts, ends, first_tile, it_start, it_end, total


def _combine_kernel(p1_ref, p2_ref, p1_next_ref, p2_next_ref, ys_hbm, route_ref, x1_ref, mod_ref, g_ref,
                    o_ref, y_scr, sem):
    i = pl.program_id(0)
    slot = i % 2

    def start_block(pa_ref, pb_ref, s):
        _start_row_gathers(ys_hbm, pa_ref, y_scr.at[s, 0], sem.at[s], MOE_TB)
        _start_row_gathers(ys_hbm, pb_ref, y_scr.at[s, 1], sem.at[s], MOE_TB)

    @pl.when(i == 0)
    def _():
        start_block(p1_ref, p2_ref, 0)

    @pl.when(i + 1 < pl.num_programs(0))
    def _():
        start_block(p1_next_ref, p2_next_ref, 1 - slot)

    _wait_row_gathers(ys_hbm, y_scr.at[slot, 0], sem.at[slot], MOE_TB)
    _wait_row_gathers(ys_hbm, y_scr.at[slot, 1], sem.at[slot], MOE_TB)
    w1 = route_ref[:, 4:5]
    w2 = route_ref[:, 5:6]
    ffn = w1 * y_scr[slot, 0] + w2 * y_scr[slot, 1]
    ms = jnp.mean(ffn * ffn, axis=-1, keepdims=True)
    y = ffn * lax.rsqrt(ms + NORM_EPS) * g_ref[...]
    o_ref[...] = x1_ref[...] + mod_ref[5:6, :] * y


def _combine(pos1, pos2, ys, route, x1, mod3, g_post):
    t_tokens = x1.shape[0]
    tb = MOE_TB
    nb = t_tokens // tb
    tiles_per_seq = SEQ // tb
    smem_blk = pl.BlockSpec((1, 1, tb), lambda i: (i, 0, 0), memory_space=pltpu.SMEM)
    smem_next = pl.BlockSpec((1, 1, tb), lambda i: (jnp.minimum(i + 1, nb - 1), 0, 0), memory_space=pltpu.SMEM)
    p1 = pos1.reshape(nb, 1, tb)
    p2 = pos2.reshape(nb, 1, tb)
    return pl.pallas_call(
        _combine_kernel,
        grid=(nb,),
        in_specs=[
            smem_blk, smem_blk, smem_next, smem_next,
            pl.BlockSpec(memory_space=pl.ANY),
            pl.BlockSpec((tb, LANES), lambda i: (i, 0)),
            pl.BlockSpec((tb, D_MODEL), lambda i: (i, 0)),
            pl.BlockSpec((None, 6, D_MODEL), lambda i: (i // tiles_per_seq, 0, 0)),
            pl.BlockSpec((1, D_MODEL), lambda i: (0, 0)),
        ],
        out_specs=pl.BlockSpec((tb, D_MODEL), lambda i: (i, 0)),
        out_shape=jax.ShapeDtypeStruct((t_tokens, D_MODEL), F32),
        scratch_shapes=[
            pltpu.VMEM((2, 2, tb, D_MODEL), F32),
            pltpu.SemaphoreType.DMA((2,)),
        ],
        compiler_params=_cparams(("arbitrary",)),
        name="combine",
    )(p1, p2, p1, p2, ys, route, x1, mod3, g_post.reshape(1, D_MODEL))


def _rope_tables():
    rows = SEQ // GRID_W
    row = jnp.repeat(jnp.arange(rows, dtype=F32), GRID_W)
    col = jnp.tile(jnp.arange(GRID_W, dtype=F32), rows)
    n_freq = HEAD_DIM // 4
    inv = ROPE_THETA ** (-jnp.arange(n_freq, dtype=F32) / n_freq)
    ang_r = row[:, None] * inv
    ang_c = col[:, None] * inv
    ang = jnp.concatenate([ang_r, ang_r, ang_c, ang_c], axis=-1)
    cos, sin = jnp.cos(ang), jnp.sin(ang)
    first = (jnp.arange(HEAD_DIM) % (2 * n_freq)) < n_freq
    sin_a = jnp.where(first, -sin, 0.0)
    sin_b = jnp.where(first, 0.0, sin)
    return cos, sin_a, sin_b


def kernel(x, c, w_ada, b_ada, g_pre_mix, g_post_mix, w_in, q_norm_g, k_norm_g, w_attn_out, w_fourier_out,
           w_mix_out, g_pre_ffn, g_post_ffn, w_group_router, b_group_router, w_expert_router, b_expert_router,
           w_exp_gate, w_exp_up, w_exp_down):
    n_b, seq, d = x.shape
    assert seq == SEQ and d == D_MODEL
    t_tokens = n_b * seq
    x2 = x.reshape(t_tokens, d)

    c_pad = jnp.zeros((8, d), F32).at[:n_b].set(c)
    mod = _ada(c_pad, w_ada, b_ada)[:n_b]
    mod3 = mod.reshape(n_b, 6, d)

    cos, sin_a, sin_b = _rope_tables()
    q, k, vt, f, gates = _in_proj(x2, mod3, g_pre_mix, _regroup_w_in(w_in), cos, sin_a, sin_b, q_norm_g, k_norm_g)
    attn = _attention(q, k, vt)
    four = _fourier(f)

    n_r = N_GROUPS + N_EXPERTS
    w_router = jnp.zeros((d, LANES), F32).at[:, :N_GROUPS].set(w_group_router).at[:, N_GROUPS:n_r].set(w_expert_router)
    b_router = jnp.zeros((1, LANES), F32).at[0, :N_GROUPS].set(b_group_router).at[0, N_GROUPS:n_r].set(b_expert_router)
    x1, h2, logits = _post_mix(
        attn, four, gates, x2, mod3, g_post_mix, g_pre_ffn,
        w_attn_out.astype(BF16),
        w_fourier_out.astype(BF16).reshape(N_FOURIER_GROUPS, FOURIER_GROUP_DIM, d),
        w_mix_out.astype(BF16), w_router.astype(BF16), b_router)

    route, cnt = _route(logits)
    counts = cnt[0, N_GROUPS:n_r].astype(jnp.int32)
    starts, ends, first_tile, it_start, it_end, total = _expert_items(counts)
    pos1 = route[:, 2].astype(jnp.int32)
    pos2 = route[:, 3].astype(jnp.int32)

    xs = _dispatch(pos1, pos2, h2)

    n_tiles = 2 * t_tokens // MOE_TM
    max_items = n_tiles + N_EXPERTS - 1
    kk = jnp.minimum(jnp.arange(max_items, dtype=jnp.int32), total - 1)
    item_exp = jnp.sum((it_end[None, :] <= kk[:, None]).astype(jnp.int32), axis=1)
    item_tile = first_tile[item_exp] + kk - it_start[item_exp]
    ys = _experts(item_tile.astype(jnp.int32), item_exp.astype(jnp.int32), starts.astype(jnp.int32),
                  ends.astype(jnp.int32), total.reshape(1).astype(jnp.int32), xs, w_exp_gate, w_exp_up, w_exp_down)

    out = _combine(pos1, pos2, ys, route, x1, mod3, g_post_ffn)
    return out.reshape(n_b, seq, d)
```

```python
import math

import numpy as np
import jax
import jax.numpy as jnp
from jax import lax
from jax.experimental import pallas as pl
from jax.experimental.pallas import tpu as pltpu

F32 = jnp.float32
BF16 = jnp.bfloat16

D_MODEL = 2048
SEQ = 4096
N_HEADS = 16
N_KV_HEADS = 4
Q_PER_KV = N_HEADS // N_KV_HEADS
HEAD_DIM = 128
ROPE_THETA = 10000.0
GRID_W = 64
ATTN_Q_DIM = N_HEADS * HEAD_DIM
ATTN_KV_DIM = N_KV_HEADS * HEAD_DIM
N_FOURIER_GROUPS = 4
FOURIER_GROUP_DIM = 256
FOURIER_DIM = N_FOURIER_GROUPS * FOURIER_GROUP_DIM
IN_PROJ_DIM = ATTN_Q_DIM + 2 * ATTN_KV_DIM + FOURIER_DIM + 2 * D_MODEL
N_GROUPS = 4
EXPERTS_PER_GROUP = 8
N_EXPERTS = N_GROUPS * EXPERTS_PER_GROUP
EXPERT_FF = 512
NORM_EPS = 1e-6

LANES = 128
FFT_RADIX = 64
FFT_STAGE2_UNROLL = 8
VMEM_LIMIT = 56 * 1024 * 1024

Q_PRESCALE = HEAD_DIM ** -0.5 * math.log2(math.e)

ADA_TN = 1024
INPROJ_TM = 512
INPROJ_GROUPS = N_KV_HEADS
INPROJ_Q_COLS = ATTN_Q_DIM // INPROJ_GROUPS
INPROJ_GATE_COLS = 2 * D_MODEL // INPROJ_GROUPS
INPROJ_GATE_SPLIT = 512
VT_ROWS = HEAD_DIM + 16
ATTN_TQ = 512
ATTN_SUB = 128
ATTN_TK = 512
ATTN_MIN_DENOM = 2.0 ** -64
POST_TM = 256
ROUTE_TB = 512
MOE_TB = 256
MOE_TM = 256
GATHER_UNROLL = 8


def _cparams(sem, vmem=VMEM_LIMIT):
    return pltpu.CompilerParams(dimension_semantics=sem, vmem_limit_bytes=vmem)


def _ada_kernel(c_ref, w_ref, b_ref, o_ref):
    cs = c_ref[...]
    s = cs * jax.nn.sigmoid(cs)
    o_ref[...] = jnp.dot(s.astype(BF16), w_ref[...].astype(BF16), preferred_element_type=F32) + b_ref[...]


def _ada(c_pad, w_ada, b_ada):
    n = w_ada.shape[1]
    return pl.pallas_call(
        _ada_kernel,
        grid=(n // ADA_TN,),
        in_specs=[
            pl.BlockSpec((8, D_MODEL), lambda j: (0, 0)),
            pl.BlockSpec((D_MODEL, ADA_TN), lambda j: (0, j)),
            pl.BlockSpec((1, ADA_TN), lambda j: (0, j)),
        ],
        out_specs=pl.BlockSpec((8, ADA_TN), lambda j: (0, j)),
        out_shape=jax.ShapeDtypeStruct((8, n), F32),
        compiler_params=_cparams(("arbitrary",)),
        name="ada",
    )(c_pad, w_ada, b_ada.reshape(1, n))


def _head_norm_rope(t, g, cos, sin_a, sin_b):
    ms = jnp.mean(t * t, axis=-1, keepdims=True)
    y = t * lax.rsqrt(ms + NORM_EPS) * g
    return y * cos + pltpu.roll(y, 96, 1) * sin_a + pltpu.roll(y, 32, 1) * sin_b


def _inproj_kernel(x_ref, mod_ref, g_ref, w_ref, cos_ref, sa_ref, sb_ref, qg_ref, kg_ref,
                   q_ref, k_ref, v_ref, f_ref, gate_ref, h_scr):
    j = pl.program_id(1)

    @pl.when(j == 0)
    def _():
        x = x_ref[...]
        ms = jnp.mean(x * x, axis=-1, keepdims=True)
        y = x * lax.rsqrt(ms + NORM_EPS) * g_ref[...]
        h_scr[...] = (y * (1.0 + mod_ref[1:2, :]) + mod_ref[0:1, :]).astype(BF16)

    h = h_scr[...]
    cos, sin_a, sin_b = cos_ref[...], sa_ref[...], sb_ref[...]
    c0 = 0
    r_q = jnp.dot(h, w_ref[:, c0:c0 + INPROJ_Q_COLS], preferred_element_type=F32)
    for hh in range(INPROJ_Q_COLS // HEAD_DIM):
        t = _head_norm_rope(r_q[:, hh * HEAD_DIM:(hh + 1) * HEAD_DIM], qg_ref[...], cos, sin_a, sin_b)
        q_ref[hh] = (t * Q_PRESCALE).astype(BF16)
    c0 += INPROJ_Q_COLS
    r_kv = jnp.dot(h, w_ref[:, c0:c0 + 2 * HEAD_DIM], preferred_element_type=F32)
    k_ref[0] = _head_norm_rope(r_kv[:, :HEAD_DIM], kg_ref[...], cos, sin_a, sin_b).astype(BF16)
    v_ref[0, 0:HEAD_DIM, :] = r_kv[:, HEAD_DIM:].T.astype(BF16)
    ones_row = lax.broadcasted_iota(jnp.int32, (VT_ROWS - HEAD_DIM, INPROJ_TM), 0) == 0
    v_ref[0, HEAD_DIM:VT_ROWS, :] = jnp.where(ones_row, 1.0, 0.0).astype(BF16)
    c0 += 2 * HEAD_DIM
    f_ref[0] = jnp.dot(h, w_ref[:, c0:c0 + FOURIER_GROUP_DIM], preferred_element_type=F32).astype(BF16)
    c0 += FOURIER_GROUP_DIM
    for s in range(INPROJ_GATE_COLS // INPROJ_GATE_SPLIT):
        r_g = jnp.dot(h, w_ref[:, c0:c0 + INPROJ_GATE_SPLIT], preferred_element_type=F32)
        gate_ref[:, s * INPROJ_GATE_SPLIT:(s + 1) * INPROJ_GATE_SPLIT] = jax.nn.sigmoid(r_g).astype(BF16)
        c0 += INPROJ_GATE_SPLIT


def _regroup_w_in(w_in):
    o1 = ATTN_Q_DIM
    o2 = o1 + ATTN_KV_DIM
    o3 = o2 + ATTN_KV_DIM
    o4 = o3 + FOURIER_DIM
    parts = []
    for j in range(INPROJ_GROUPS):
        parts += [
            w_in[:, j * INPROJ_Q_COLS:(j + 1) * INPROJ_Q_COLS],
            w_in[:, o1 + j * HEAD_DIM:o1 + (j + 1) * HEAD_DIM],
            w_in[:, o2 + j * HEAD_DIM:o2 + (j + 1) * HEAD_DIM],
            w_in[:, o3 + j * FOURIER_GROUP_DIM:o3 + (j + 1) * FOURIER_GROUP_DIM],
            w_in[:, o4 + j * INPROJ_GATE_COLS:o4 + (j + 1) * INPROJ_GATE_COLS],
        ]
    return jnp.concatenate(parts, axis=1).astype(BF16)


def _in_proj(x2, mod3, g_pre, w_grouped, cos, sin_a, sin_b, qg, kg):
    t_tokens = x2.shape[0]
    tm = INPROJ_TM
    tn = IN_PROJ_DIM // INPROJ_GROUPS
    tiles_per_seq = SEQ // tm
    row_tbl = pl.BlockSpec((tm, HEAD_DIM), lambda i, j: (i % tiles_per_seq, 0))
    vec_hd = pl.BlockSpec((1, HEAD_DIM), lambda i, j: (0, 0))
    return pl.pallas_call(
        _inproj_kernel,
        grid=(t_tokens // tm, INPROJ_GROUPS),
        in_specs=[
            pl.BlockSpec((tm, D_MODEL), lambda i, j: (i, 0)),
            pl.BlockSpec((None, 6, D_MODEL), lambda i, j: (i // tiles_per_seq, 0, 0)),
            pl.BlockSpec((1, D_MODEL), lambda i, j: (0, 0)),
            pl.BlockSpec((D_MODEL, tn), lambda i, j: (0, j)),
            row_tbl, row_tbl, row_tbl, vec_hd, vec_hd,
        ],
        out_specs=[
            pl.BlockSpec((INPROJ_Q_COLS // HEAD_DIM, tm, HEAD_DIM), lambda i, j: (j, i, 0)),
            pl.BlockSpec((1, tm, HEAD_DIM), lambda i, j: (j, i, 0)),
            pl.BlockSpec((1, VT_ROWS, tm), lambda i, j: (j, 0, i)),
            pl.BlockSpec((1, tm, FOURIER_GROUP_DIM), lambda i, j: (j, i, 0)),
            pl.BlockSpec((tm, INPROJ_GATE_COLS), lambda i, j: (i, j)),
        ],
        out_shape=[
            jax.ShapeDtypeStruct((N_HEADS, t_tokens, HEAD_DIM), BF16),
            jax.ShapeDtypeStruct((N_KV_HEADS, t_tokens, HEAD_DIM), BF16),
            jax.ShapeDtypeStruct((N_KV_HEADS, VT_ROWS, t_tokens), BF16),
            jax.ShapeDtypeStruct((N_FOURIER_GROUPS, t_tokens, FOURIER_GROUP_DIM), BF16),
            jax.ShapeDtypeStruct((t_tokens, 2 * D_MODEL), BF16),
        ],
        scratch_shapes=[pltpu.VMEM((tm, D_MODEL), BF16)],
        compiler_params=_cparams(("arbitrary", "arbitrary")),
        name="in_proj",
    )(x2, mod3, g_pre.reshape(1, D_MODEL), w_grouped, cos, sin_a, sin_b,
      qg.reshape(1, HEAD_DIM), kg.reshape(1, HEAD_DIM))


def _attn_kernel(q_ref, k_ref, vt_ref, o_ref, kmax_scr, m_scr, acc_scr):
    n_chunks = SEQ // ATTN_TK
    n_sub = ATTN_TQ // ATTN_SUB
    n_q = Q_PER_KV * ATTN_SUB
    qk_dims = (((1,), (1,)), ((), ()))

    def load_q(sub):
        return q_ref[:, sub * ATTN_SUB:(sub + 1) * ATTN_SUB, :].reshape(n_q, HEAD_DIM)

    def store_out(sub, acc):
        o_t = acc[:HEAD_DIM] * (1.0 / acc[HEAD_DIM:HEAD_DIM + 1])
        r0 = sub * ATTN_SUB
        for hh in range(Q_PER_KV):
            o_ref[r0:r0 + ATTN_SUB, hh * HEAD_DIM:(hh + 1) * HEAD_DIM] = (
                o_t[:, hh * ATTN_SUB:(hh + 1) * ATTN_SUB].T.astype(BF16))

    @pl.when(pl.program_id(2) == 0)
    def _():
        kf = k_ref[...].astype(F32)
        kn2 = jnp.sum(kf * kf, axis=1, keepdims=True)
        kmax_scr[...] = jnp.broadcast_to(jnp.max(kn2, axis=0, keepdims=True), kmax_scr.shape)

    qs = [load_q(sub) for sub in range(n_sub)]
    ones = jnp.ones((8, HEAD_DIM), BF16)
    shift = []
    for sub in range(n_sub):
        qf = qs[sub].astype(F32)
        qn2 = lax.dot_general(ones, (qf * qf).astype(BF16), qk_dims, preferred_element_type=F32)
        shift.append(jnp.sqrt(qn2[0:1] * kmax_scr[0:1, 0:1]))
    def scores(c, sub):
        k_c = k_ref[c * ATTN_TK:(c + 1) * ATTN_TK, :]
        return lax.dot_general(k_c, qs[sub], qk_dims, preferred_element_type=F32)

    units = [(c, sub) for c in range(n_chunks) for sub in range(n_sub)]
    acc = [None] * n_sub
    s_next = scores(*units[0])
    for u, (c, sub) in enumerate(units):
        s = s_next
        if u + 1 < len(units):
            s_next = scores(*units[u + 1])
        vt_c = vt_ref[:, c * ATTN_TK:(c + 1) * ATTN_TK]
        d = jnp.dot(vt_c, jnp.exp2(s - shift[sub]).astype(BF16), preferred_element_type=F32)
        acc[sub] = d if c == 0 else acc[sub] + d
    den_min = None
    for sub in range(n_sub):
        store_out(sub, acc[sub])
        den = acc[sub][HEAD_DIM:HEAD_DIM + 1]
        den_min = den if den_min is None else jnp.minimum(den_min, den)
    accurate = jnp.min(den_min) >= ATTN_MIN_DENOM

    @pl.when(jnp.logical_not(accurate))
    def _():
        for sub in range(n_sub):
            q = load_q(sub)
            m_scr[...] = jnp.full(m_scr.shape, -jnp.inf, F32)
            acc_scr[...] = jnp.zeros_like(acc_scr)

            def body(c, carry):
                off = pl.multiple_of(c * ATTN_TK, ATTN_TK)
                s = lax.dot_general(k_ref[pl.ds(off, ATTN_TK), :], q, qk_dims, preferred_element_type=F32)
                m_old = m_scr[...]
                m_new = jnp.maximum(m_old, jnp.max(s, axis=0, keepdims=True))
                p = jnp.exp2(s - m_new).astype(BF16)
                acc_scr[...] = (jnp.exp2(m_old - m_new) * acc_scr[...]
                                + jnp.dot(vt_ref[:, pl.ds(off, ATTN_TK)], p, preferred_element_type=F32))
                m_scr[...] = m_new
                return carry

            lax.fori_loop(0, n_chunks, body, 0)
            store_out(sub, acc_scr[...])


def _attention(q, k, vt):
    t_tokens = q.shape[1]
    n_b = t_tokens // SEQ
    qt = SEQ // ATTN_TQ
    return pl.pallas_call(
        _attn_kernel,
        grid=(n_b, N_KV_HEADS, qt),
        in_specs=[
            pl.BlockSpec((Q_PER_KV, ATTN_TQ, HEAD_DIM), lambda b, h, i: (h, b * qt + i, 0)),
            pl.BlockSpec((None, SEQ, HEAD_DIM), lambda b, h, i: (h, b, 0)),
            pl.BlockSpec((None, VT_ROWS, SEQ), lambda b, h, i: (h, 0, b)),
        ],
        out_specs=pl.BlockSpec((ATTN_TQ, Q_PER_KV * HEAD_DIM), lambda b, h, i: (b * qt + i, h)),
        out_shape=jax.ShapeDtypeStruct((t_tokens, ATTN_Q_DIM), BF16),
        scratch_shapes=[
            pltpu.VMEM((8, LANES), F32),
            pltpu.VMEM((1, Q_PER_KV * ATTN_SUB), F32),
            pltpu.VMEM((VT_ROWS, Q_PER_KV * ATTN_SUB), F32),
        ],
        compiler_params=_cparams(("arbitrary", "arbitrary", "arbitrary")),
        name="attention",
    )(q, k, vt)


def _fourier_tables():
    r = FFT_RADIX
    n = FOURIER_GROUP_DIM
    ch = np.arange(n)
    ang_c = 2.0 * np.pi * np.outer(ch, ch) / n
    scale = 1.0 / math.sqrt(SEQ * n)
    w_ch = np.concatenate([np.cos(ang_c), -np.sin(ang_c)], axis=1) * scale
    a = np.arange(r)
    ang1 = 2.0 * np.pi * np.outer(a, a) / r
    c1, s1 = np.cos(ang1), np.sin(ang1)
    w1 = np.zeros((r, 2, 2, r))
    w1[:, 0, 0, :] = c1
    w1[:, 0, 1, :] = s1
    w1[:, 1, 0, :] = -s1
    w1[:, 1, 1, :] = c1
    w1 = w1.reshape(2 * r, 2 * r)
    sp = (np.arange(r)[:, None] + r * np.arange(r)[None, :])
    th = 2.0 * np.pi * sp[:, :, None] * np.arange(r)[None, None, :] / SEQ
    w2 = np.concatenate([np.cos(th), np.sin(th)], axis=2)
    return tuple(jnp.asarray(t, dtype=F32).astype(BF16) for t in (w_ch, w1, w2))


def _fourier_kernel(f_ref, wch_ref, w1_ref, w2_ref, o_ref, x_scr, y_scr, o_scr):
    r = FFT_RADIX
    n = FOURIER_GROUP_DIM
    z = jnp.dot(f_ref[...], wch_ref[...], preferred_element_type=F32)
    x_scr[0:r] = z[:, :n].astype(BF16).reshape(r, r, n)
    x_scr[r:2 * r] = z[:, n:].astype(BF16).reshape(r, r, n)
    y = lax.dot_general(w1_ref[...], x_scr[...], (((1,), (0,)), ((), ())),
                        preferred_element_type=F32)
    y_scr[...] = y.astype(BF16)

    def body(c, carry):
        slab = y_scr[pl.ds(2 * c, 2)].reshape(2 * r, n)
        res = jnp.dot(w2_ref[c], slab, preferred_element_type=F32)
        for s in range(n // LANES):
            o_scr[s, pl.ds(c, r, stride=r), :] = res[:, s * LANES:(s + 1) * LANES]
        return carry

    lax.fori_loop(0, r, body, 0, unroll=FFT_STAGE2_UNROLL)
    for s in range(n // LANES):
        o_ref[:, s * LANES:(s + 1) * LANES] = o_scr[s].astype(BF16)


def _fourier(f):
    t_tokens = f.shape[1]
    n_b = t_tokens // SEQ
    r = FFT_RADIX
    n = FOURIER_GROUP_DIM
    w_ch, w1, w2 = _fourier_tables()
    blk = pl.BlockSpec((None, SEQ, n), lambda g, b: (g, b, 0))
    return pl.pallas_call(
        _fourier_kernel,
        grid=(N_FOURIER_GROUPS, n_b),
        in_specs=[
            blk,
            pl.BlockSpec((n, 2 * n), lambda g, b: (0, 0)),
            pl.BlockSpec((2 * r, 2 * r), lambda g, b: (0, 0)),
            pl.BlockSpec((r, r, 2 * r), lambda g, b: (0, 0, 0)),
        ],
        out_specs=blk,
        out_shape=jax.ShapeDtypeStruct((N_FOURIER_GROUPS, t_tokens, n), BF16),
        scratch_shapes=[
            pltpu.VMEM((2 * r, r, n), BF16),
            pltpu.VMEM((2 * r, r, n), BF16),
            pltpu.VMEM((n // LANES, SEQ, LANES), F32),
        ],
        compiler_params=_cparams(("arbitrary", "arbitrary")),
        name="fourier",
    )(f, w_ch, w1, w2)


def _post_mix_kernel(attn_ref, four_ref, gate_ref, x_ref, mod_ref, gpost_ref, gpre_ref,
                     wao_ref, wfo_ref, wmo_ref, wr_ref, br_ref,
                     x1_ref, h2_ref, logit_ref):
    a = jnp.dot(attn_ref[...], wao_ref[...], preferred_element_type=F32)
    fo = jnp.dot(four_ref[0], wfo_ref[0], preferred_element_type=F32)
    for g in range(1, N_FOURIER_GROUPS):
        fo = fo + jnp.dot(four_ref[g], wfo_ref[g], preferred_element_type=F32)
    ga = gate_ref[:, :D_MODEL].astype(F32)
    gf = gate_ref[:, D_MODEL:].astype(F32)
    m = (ga * a + gf * fo).astype(BF16)
    mixed = jnp.dot(m, wmo_ref[...], preferred_element_type=F32)
    ms = jnp.mean(mixed * mixed, axis=-1, keepdims=True)
    y = mixed * lax.rsqrt(ms + NORM_EPS) * gpost_ref[...]
    x1 = x_ref[...] + mod_ref[2:3, :] * y
    x1_ref[...] = x1
    ms1 = jnp.mean(x1 * x1, axis=-1, keepdims=True)
    h2 = (x1 * lax.rsqrt(ms1 + NORM_EPS) * gpre_ref[...]) * (1.0 + mod_ref[4:5, :]) + mod_ref[3:4, :]
    h2_ref[...] = h2
    logit_ref[...] = jnp.dot(h2.astype(BF16), wr_ref[...], preferred_element_type=F32) + br_ref[...]


def _post_mix(attn, four, gates, x2, mod3, g_post, g_pre, wao, wfo4, wmo, w_router, b_router):
    t_tokens = x2.shape[0]
    tm = POST_TM
    tiles_per_seq = SEQ // tm
    n = FOURIER_GROUP_DIM
    const2 = lambda i: (0, 0)
    resident = dict(pipeline_mode=pl.Buffered(1))
    return pl.pallas_call(
        _post_mix_kernel,
        grid=(t_tokens // tm,),
        in_specs=[
            pl.BlockSpec((tm, ATTN_Q_DIM), lambda i: (i, 0)),
            pl.BlockSpec((N_FOURIER_GROUPS, tm, n), lambda i: (0, i, 0)),
            pl.BlockSpec((tm, 2 * D_MODEL), lambda i: (i, 0)),
            pl.BlockSpec((tm, D_MODEL), lambda i: (i, 0)),
            pl.BlockSpec((None, 6, D_MODEL), lambda i: (i // tiles_per_seq, 0, 0)),
            pl.BlockSpec((1, D_MODEL), const2),
            pl.BlockSpec((1, D_MODEL), const2),
            pl.BlockSpec((ATTN_Q_DIM, D_MODEL), const2, **resident),
            pl.BlockSpec((N_FOURIER_GROUPS, n, D_MODEL), lambda i: (0, 0, 0), **resident),
            pl.BlockSpec((D_MODEL, D_MODEL), const2, **resident),
            pl.BlockSpec((D_MODEL, LANES), const2, **resident),
            pl.BlockSpec((1, LANES), const2),
        ],
        out_specs=[
            pl.BlockSpec((tm, D_MODEL), lambda i: (i, 0)),
            pl.BlockSpec((tm, D_MODEL), lambda i: (i, 0)),
            pl.BlockSpec((tm, LANES), lambda i: (i, 0)),
        ],
        out_shape=[
            jax.ShapeDtypeStruct((t_tokens, D_MODEL), F32),
            jax.ShapeDtypeStruct((t_tokens, D_MODEL), F32),
            jax.ShapeDtypeStruct((t_tokens, LANES), F32),
        ],
        compiler_params=_cparams(("arbitrary",)),
        name="post_mix",
    )(attn, four, gates, x2, mod3, g_post.reshape(1, D_MODEL), g_pre.reshape(1, D_MODEL),
      wao, wfo4, wmo, w_router, b_router)


def _first_argmax(vals, lane):
    mx = jnp.max(vals, axis=-1, keepdims=True)
    idx = jnp.min(jnp.where(vals == mx, lane, float(LANES)), axis=-1, keepdims=True)
    return mx, idx


def _route_kernel(logit_ref, tri_ref, out_ref, cnt_ref, carry_scr, start_scr):
    phase = pl.program_id(0)
    i = pl.program_id(1)

    @pl.when((phase == 0) & (i == 0))
    def _():
        carry_scr[...] = jnp.zeros_like(carry_scr)
        start_scr[...] = jnp.zeros_like(start_scr)

    @pl.when((phase == 1) & (i == 0))
    def _():
        counts = carry_scr[...]
        lane8 = lax.broadcasted_iota(jnp.int32, counts.shape, 1)
        incl = counts
        shift = 1
        while shift < N_EXPERTS:
            incl = incl + jnp.where(lane8 >= N_GROUPS + shift, pltpu.roll(incl, shift, 1), 0.0)
            shift *= 2
        start_scr[...] = incl - counts
        carry_scr[...] = jnp.zeros_like(carry_scr)

    lg = logit_ref[...]
    lane = lax.broadcasted_iota(jnp.int32, lg.shape, 1).astype(F32)
    neg = jnp.float32(-jnp.inf)
    gl = jnp.where(lane < N_GROUPS, lg, neg)
    gmax, gidx = _first_argmax(gl, lane)
    g_w = 1.0 / jnp.sum(jnp.exp(gl - gmax), axis=-1, keepdims=True)
    lo = N_GROUPS + EXPERTS_PER_GROUP * gidx
    el = jnp.where((lane >= lo) & (lane < lo + EXPERTS_PER_GROUP), lg, neg)
    m1, i1 = _first_argmax(el, lane)
    el2 = jnp.where(lane == i1, neg, el)
    m2, i2 = _first_argmax(el2, lane)
    p2 = jnp.exp(m2 - m1)
    w1 = g_w / (1.0 + p2)
    w2 = g_w * p2 / (1.0 + p2)
    oh = jnp.where((lane == i1) | (lane == i2), 1.0, 0.0)
    before = (jnp.dot(tri_ref[...], oh.astype(BF16), preferred_element_type=F32)
              + carry_scr[0:1, :] + start_scr[0:1, :])
    pos1 = jnp.sum(jnp.where(lane == i1, before, 0.0), axis=-1, keepdims=True)
    pos2 = jnp.sum(jnp.where(lane == i2, before, 0.0), axis=-1, keepdims=True)
    carry_scr[...] = carry_scr[...] + jnp.sum(oh, axis=0, keepdims=True)
    out = jnp.where(lane == 0, i1 - N_GROUPS, 0.0)
    out = jnp.where(lane == 1, i2 - N_GROUPS, out)
    out = jnp.where(lane == 2, pos1, out)
    out = jnp.where(lane == 3, pos2, out)
    out = jnp.where(lane == 4, w1, out)
    out = jnp.where(lane == 5, w2, out)
    out_ref[...] = out
    cnt_ref[...] = carry_scr[...]


def _route(logits):
    t_tokens = logits.shape[0]
    tb = ROUTE_TB
    tri = jnp.asarray(np.tril(np.ones((tb, tb), np.float32), -1), dtype=BF16)
    return pl.pallas_call(
        _route_kernel,
        grid=(2, t_tokens // tb),
        in_specs=[
            pl.BlockSpec((tb, LANES), lambda p, i: (i, 0)),
            pl.BlockSpec((tb, tb), lambda p, i: (0, 0)),
        ],
        out_specs=[
            pl.BlockSpec((tb, LANES), lambda p, i: (i * p, 0)),
            pl.BlockSpec((8, LANES), lambda p, i: (0, 0)),
        ],
        out_shape=[
            jax.ShapeDtypeStruct((t_tokens, LANES), F32),
            jax.ShapeDtypeStruct((8, LANES), F32),
        ],
        scratch_shapes=[pltpu.VMEM((8, LANES), F32), pltpu.VMEM((8, LANES), F32)],
        compiler_params=_cparams(("arbitrary", "arbitrary")),
        name="route",
    )(logits, tri)


def _row_gather(src_hbm, src_row, j, dst, sem):
    return pltpu.make_async_copy(src_hbm.at[pl.ds(src_row, 1), :], dst.at[pl.ds(j, 1), :], sem)


def _start_row_gathers(src_hbm, idx_ref, dst, sem, n_rows):
    def body(jj, carry):
        for u in range(GATHER_UNROLL):
            j = jj * GATHER_UNROLL + u
            _row_gather(src_hbm, idx_ref[0, 0, j], j, dst, sem).start(priority=u % 2)
        return carry

    lax.fori_loop(0, n_rows // GATHER_UNROLL, body, 0)


def _wait_row_gathers(src_hbm, dst, sem, n_rows):
    def body(jj, carry):
        for u in range(GATHER_UNROLL):
            _row_gather(src_hbm, 0, jj * GATHER_UNROLL + u, dst, sem).wait()
        return carry

    lax.fori_loop(0, n_rows // GATHER_UNROLL, body, 0)


def _row_scatter(src, j, dst_hbm, dst_row, sem):
    return pltpu.make_async_copy(src.at[pl.ds(j, 1), :], dst_hbm.at[pl.ds(dst_row, 1), :], sem)


def _dispatch_kernel(p1_ref, p2_ref, h_ref, xs_hbm, buf, sem):
    i = pl.program_id(0)
    slot = i % 2
    src = buf.at[slot]
    src[...] = h_ref[...]

    def start(jj, carry):
        for u in range(GATHER_UNROLL):
            j = jj * GATHER_UNROLL + u
            _row_scatter(src, j, xs_hbm, p1_ref[0, 0, j], sem.at[slot]).start(priority=0)
            _row_scatter(src, j, xs_hbm, p2_ref[0, 0, j], sem.at[slot]).start(priority=1)
        return carry

    lax.fori_loop(0, MOE_TB // GATHER_UNROLL, start, 0)

    def wait_block(s):
        def body(jj, carry):
            for u in range(GATHER_UNROLL):
                j = jj * GATHER_UNROLL + u
                _row_scatter(buf.at[s], j, xs_hbm, 0, sem.at[s]).wait()
                _row_scatter(buf.at[s], j, xs_hbm, 0, sem.at[s]).wait()
            return carry

        lax.fori_loop(0, MOE_TB // GATHER_UNROLL, body, 0)

    @pl.when(i > 0)
    def _():
        wait_block(1 - slot)

    @pl.when(i + 1 == pl.num_programs(0))
    def _():
        wait_block(slot)


def _dispatch(pos1, pos2, h2):
    t_tokens = h2.shape[0]
    nb = t_tokens // MOE_TB
    smem_blk = pl.BlockSpec((1, 1, MOE_TB), lambda i: (i, 0, 0), memory_space=pltpu.SMEM)
    return pl.pallas_call(
        _dispatch_kernel,
        grid=(nb,),
        in_specs=[smem_blk, smem_blk, pl.BlockSpec((MOE_TB, D_MODEL), lambda i: (i, 0))],
        out_specs=pl.BlockSpec(memory_space=pl.ANY),
        out_shape=jax.ShapeDtypeStruct((2 * t_tokens, D_MODEL), F32),
        scratch_shapes=[pltpu.VMEM((2, MOE_TB, D_MODEL), F32), pltpu.SemaphoreType.DMA((2,))],
        compiler_params=_cparams(("arbitrary",)),
        name="dispatch",
    )(pos1.reshape(nb, 1, MOE_TB), pos2.reshape(nb, 1, MOE_TB), h2)


def _expert_kernel(tile_ref, exp_ref, start_ref, end_ref, nitem_ref,
                   xs_ref, wg_ref, wu_ref, wd_ref, ys_ref, wg_scr, wu_scr, wd_scr):
    k = pl.program_id(0)
    tile = tile_ref[k]
    e = exp_ref[k]
    prev = jnp.maximum(k - 1, 0)
    first_visit = (k == 0) | (tile != tile_ref[prev])
    new_expert = (k == 0) | (e != exp_ref[prev])

    @pl.when(new_expert)
    def _():
        wg_scr[...] = wg_ref[...].astype(BF16)
        wu_scr[...] = wu_ref[...].astype(BF16)
        wd_scr[...] = wd_ref[...].astype(BF16)

    @pl.when(first_visit)
    def _():
        ys_ref[...] = jnp.zeros_like(ys_ref)

    @pl.when(k < nitem_ref[0])
    def _():
        x = xs_ref[...].astype(BF16)
        g = jnp.dot(x, wg_scr[...], preferred_element_type=F32)
        u = jnp.dot(x, wu_scr[...], preferred_element_type=F32)
        hmid = (g * jax.nn.sigmoid(g) * u).astype(BF16)
        y = jnp.dot(hmid, wd_scr[...], preferred_element_type=F32)
        row = tile * MOE_TM + lax.broadcasted_iota(jnp.int32, (MOE_TM, 1), 0)
        valid = (row >= start_ref[e]) & (row < end_ref[e])
        ys_ref[...] += jnp.where(valid, y, 0.0)


def _experts(item_tile, item_exp, starts, ends, n_items, xs, wg, wu, wd):
    n_rows = xs.shape[0]
    n_tiles = n_rows // MOE_TM
    max_items = n_tiles + N_EXPERTS - 1
    grid_spec = pltpu.PrefetchScalarGridSpec(
        num_scalar_prefetch=5,
        grid=(max_items,),
        in_specs=[
            pl.BlockSpec((MOE_TM, D_MODEL), lambda k, t, e, s, en, n: (t[k], 0)),
            pl.BlockSpec((None, D_MODEL, EXPERT_FF), lambda k, t, e, s, en, n: (e[k], 0, 0)),
            pl.BlockSpec((None, D_MODEL, EXPERT_FF), lambda k, t, e, s, en, n: (e[k], 0, 0)),
            pl.BlockSpec((None, EXPERT_FF, D_MODEL), lambda k, t, e, s, en, n: (e[k], 0, 0)),
        ],
        out_specs=pl.BlockSpec((MOE_TM, D_MODEL), lambda k, t, e, s, en, n: (t[k], 0)),
        scratch_shapes=[
            pltpu.VMEM((D_MODEL, EXPERT_FF), BF16),
            pltpu.VMEM((D_MODEL, EXPERT_FF), BF16),
            pltpu.VMEM((EXPERT_FF, D_MODEL), BF16),
        ],
    )
    return pl.pallas_call(
        _expert_kernel,
        grid_spec=grid_spec,
        out_shape=jax.ShapeDtypeStruct((n_rows, D_MODEL), F32),
        compiler_params=_cparams(("arbitrary",)),
        name="experts",
    )(item_tile, item_exp, starts, ends, n_items, xs, wg, wu, wd)


def _expert_items(counts):
    ends = jnp.cumsum(counts)
    starts = ends - counts
    first_tile = starts // MOE_TM
    last_tile = (ends - 1) // MOE_TM
    n_it = jnp.where(counts > 0, last_tile - first_tile + 1, 0)
    it_end = jnp.cumsum(n_it)
    it_start = it_end - n_it
    total = it_end[-1]
    return starts, ends, first_tile, it_start, it_end, total


def _combine_kernel(p1_ref, p2_ref, p1_next_ref, p2_next_ref, ys_hbm, route_ref, x1_ref, mod_ref, g_ref,
                    o_ref, y_scr, sem):
    i = pl.program_id(0)
    slot = i % 2

    def start_block(pa_ref, pb_ref, s):
        _start_row_gathers(ys_hbm, pa_ref, y_scr.at[s, 0], sem.at[s], MOE_TB)
        _start_row_gathers(ys_hbm, pb_ref, y_scr.at[s, 1], sem.at[s], MOE_TB)

    @pl.when(i == 0)
    def _():
        start_block(p1_ref, p2_ref, 0)

    @pl.when(i + 1 < pl.num_programs(0))
    def _():
        start_block(p1_next_ref, p2_next_ref, 1 - slot)

    _wait_row_gathers(ys_hbm, y_scr.at[slot, 0], sem.at[slot], MOE_TB)
    _wait_row_gathers(ys_hbm, y_scr.at[slot, 1], sem.at[slot], MOE_TB)
    w1 = route_ref[:, 4:5]
    w2 = route_ref[:, 5:6]
    ffn = w1 * y_scr[slot, 0] + w2 * y_scr[slot, 1]
    ms = jnp.mean(ffn * ffn, axis=-1, keepdims=True)
    y = ffn * lax.rsqrt(ms + NORM_EPS) * g_ref[...]
    o_ref[...] = x1_ref[...] + mod_ref[5:6, :] * y


def _combine(pos1, pos2, ys, route, x1, mod3, g_post):
    t_tokens = x1.shape[0]
    tb = MOE_TB
    nb = t_tokens // tb
    tiles_per_seq = SEQ // tb
    smem_blk = pl.BlockSpec((1, 1, tb), lambda i: (i, 0, 0), memory_space=pltpu.SMEM)
    smem_next = pl.BlockSpec((1, 1, tb), lambda i: (jnp.minimum(i + 1, nb - 1), 0, 0), memory_space=pltpu.SMEM)
    p1 = pos1.reshape(nb, 1, tb)
    p2 = pos2.reshape(nb, 1, tb)
    return pl.pallas_call(
        _combine_kernel,
        grid=(nb,),
        in_specs=[
            smem_blk, smem_blk, smem_next, smem_next,
            pl.BlockSpec(memory_space=pl.ANY),
            pl.BlockSpec((tb, LANES), lambda i: (i, 0)),
            pl.BlockSpec((tb, D_MODEL), lambda i: (i, 0)),
            pl.BlockSpec((None, 6, D_MODEL), lambda i: (i // tiles_per_seq, 0, 0)),
            pl.BlockSpec((1, D_MODEL), lambda i: (0, 0)),
        ],
        out_specs=pl.BlockSpec((tb, D_MODEL), lambda i: (i, 0)),
        out_shape=jax.ShapeDtypeStruct((t_tokens, D_MODEL), F32),
        scratch_shapes=[
            pltpu.VMEM((2, 2, tb, D_MODEL), F32),
            pltpu.SemaphoreType.DMA((2,)),
        ],
        compiler_params=_cparams(("arbitrary",)),
        name="combine",
    )(p1, p2, p1, p2, ys, route, x1, mod3, g_post.reshape(1, D_MODEL))


def _rope_tables():
    rows = SEQ // GRID_W
    row = jnp.repeat(jnp.arange(rows, dtype=F32), GRID_W)
    col = jnp.tile(jnp.arange(GRID_W, dtype=F32), rows)
    n_freq = HEAD_DIM // 4
    inv = ROPE_THETA ** (-jnp.arange(n_freq, dtype=F32) / n_freq)
    ang_r = row[:, None] * inv
    ang_c = col[:, None] * inv
    ang = jnp.concatenate([ang_r, ang_r, ang_c, ang_c], axis=-1)
    cos, sin = jnp.cos(ang), jnp.sin(ang)
    first = (jnp.arange(HEAD_DIM) % (2 * n_freq)) < n_freq
    sin_a = jnp.where(first, -sin, 0.0)
    sin_b = jnp.where(first, 0.0, sin)
    return cos, sin_a, sin_b


def kernel(x, c, w_ada, b_ada, g_pre_mix, g_post_mix, w_in, q_norm_g, k_norm_g, w_attn_out, w_fourier_out,
           w_mix_out, g_pre_ffn, g_post_ffn, w_group_router, b_group_router, w_expert_router, b_expert_router,
           w_exp_gate, w_exp_up, w_exp_down):
    n_b, seq, d = x.shape
    assert seq == SEQ and d == D_MODEL
    t_tokens = n_b * seq
    x2 = x.reshape(t_tokens, d)

    c_pad = jnp.zeros((8, d), F32).at[:n_b].set(c)
    mod = _ada(c_pad, w_ada, b_ada)[:n_b]
    mod3 = mod.reshape(n_b, 6, d)

    cos, sin_a, sin_b = _rope_tables()
    q, k, vt, f, gates = _in_proj(x2, mod3, g_pre_mix, _regroup_w_in(w_in), cos, sin_a, sin_b, q_norm_g, k_norm_g)
    attn = _attention(q, k, vt)
    four = _fourier(f)

    n_r = N_GROUPS + N_EXPERTS
    w_router = jnp.zeros((d, LANES), F32).at[:, :N_GROUPS].set(w_group_router).at[:, N_GROUPS:n_r].set(w_expert_router)
    b_router = jnp.zeros((1, LANES), F32).at[0, :N_GROUPS].set(b_group_router).at[0, N_GROUPS:n_r].set(b_expert_router)
    x1, h2, logits = _post_mix(
        attn, four, gates, x2, mod3, g_post_mix, g_pre_ffn,
        w_attn_out.astype(BF16),
        w_fourier_out.astype(BF16).reshape(N_FOURIER_GROUPS, FOURIER_GROUP_DIM, d),
        w_mix_out.astype(BF16), w_router.astype(BF16), b_router)

    route, cnt = _route(logits)
    counts = cnt[0, N_GROUPS:n_r].astype(jnp.int32)
    starts, ends, first_tile, it_start, it_end, total = _expert_items(counts)
    pos1 = route[:, 2].astype(jnp.int32)
    pos2 = route[:, 3].astype(jnp.int32)

    xs = _dispatch(pos1, pos2, h2)

    n_tiles = 2 * t_tokens // MOE_TM
    max_items = n_tiles + N_EXPERTS - 1
    kk = jnp.minimum(jnp.arange(max_items, dtype=jnp.int32), total - 1)
    item_exp = jnp.sum((it_end[None, :] <= kk[:, None]).astype(jnp.int32), axis=1)
    item_tile = first_tile[item_exp] + kk - it_start[item_exp]
    ys = _experts(item_tile.astype(jnp.int32), item_exp.astype(jnp.int32), starts.astype(jnp.int32),
                  ends.astype(jnp.int32), total.reshape(1).astype(jnp.int32), xs, w_exp_gate, w_exp_up, w_exp_down)

    out = _combine(pos1, pos2, ys, route, x1, mod3, g_post_ffn)
    return out.reshape(n_b, seq, d)
```

```python
import math

import numpy as np
import jax
import jax.numpy as jnp
from jax import lax
from jax.experimental import pallas as pl
from jax.experimental.pallas import tpu as pltpu

F32 = jnp.float32
BF16 = jnp.bfloat16

D_MODEL = 2048
SEQ = 4096
N_HEADS = 16
N_KV_HEADS = 4
Q_PER_KV = N_HEADS // N_KV_HEADS
HEAD_DIM = 128
ROPE_THETA = 10000.0
GRID_W = 64
ATTN_Q_DIM = N_HEADS * HEAD_DIM
ATTN_KV_DIM = N_KV_HEADS * HEAD_DIM
N_FOURIER_GROUPS = 4
FOURIER_GROUP_DIM = 256
FOURIER_DIM = N_FOURIER_GROUPS * FOURIER_GROUP_DIM
IN_PROJ_DIM = ATTN_Q_DIM + 2 * ATTN_KV_DIM + FOURIER_DIM + 2 * D_MODEL
N_GROUPS = 4
EXPERTS_PER_GROUP = 8
N_EXPERTS = N_GROUPS * EXPERTS_PER_GROUP
EXPERT_FF = 512
NORM_EPS = 1e-6

LANES = 128
FFT_RADIX = 64
FFT_STAGE2_UNROLL = 8
VMEM_LIMIT = 56 * 1024 * 1024

Q_PRESCALE = HEAD_DIM ** -0.5 * math.log2(math.e)

ADA_TN = 1024
INPROJ_TM = 512
INPROJ_GROUPS = N_KV_HEADS
INPROJ_Q_COLS = ATTN_Q_DIM // INPROJ_GROUPS
INPROJ_GATE_COLS = 2 * D_MODEL // INPROJ_GROUPS
INPROJ_GATE_SPLIT = 512
VT_ROWS = HEAD_DIM + 16
ATTN_TQ = 512
ATTN_SUB = 128
ATTN_TK = 512
ATTN_MIN_DENOM = 2.0 ** -64
POST_TM = 256
ROUTE_TB = 1024
MOE_TB = 256
MOE_TM = 512
GATHER_UNROLL = 8


def _cparams(sem, vmem=VMEM_LIMIT):
    return pltpu.CompilerParams(dimension_semantics=sem, vmem_limit_bytes=vmem)


def _ada_kernel(c_ref, w_ref, b_ref, o_ref):
    cs = c_ref[...]
    s = cs * jax.nn.sigmoid(cs)
    o_ref[...] = jnp.dot(s.astype(BF16), w_ref[...].astype(BF16), preferred_element_type=F32) + b_ref[...]


def _ada(c_pad, w_ada, b_ada):
    n = w_ada.shape[1]
    return pl.pallas_call(
        _ada_kernel,
        grid=(n // ADA_TN,),
        in_specs=[
            pl.BlockSpec((8, D_MODEL), lambda j: (0, 0)),
            pl.BlockSpec((D_MODEL, ADA_TN), lambda j: (0, j)),
            pl.BlockSpec((1, ADA_TN), lambda j: (0, j)),
        ],
        out_specs=pl.BlockSpec((8, ADA_TN), lambda j: (0, j)),
        out_shape=jax.ShapeDtypeStruct((8, n), F32),
        compiler_params=_cparams(("arbitrary",)),
        name="ada",
    )(c_pad, w_ada, b_ada.reshape(1, n))


def _head_norm_rope(t, g, cos, sin_a, sin_b):
    ms = jnp.mean(t * t, axis=-1, keepdims=True)
    y = t * lax.rsqrt(ms + NORM_EPS) * g
    return y * cos + pltpu.roll(y, 96, 1) * sin_a + pltpu.roll(y, 32, 1) * sin_b


def _inproj_kernel(x_ref, mod_ref, g_ref, w_ref, cos_ref, sa_ref, sb_ref, qg_ref, kg_ref,
                   q_ref, k_ref, v_ref, f_ref, gate_ref, h_scr):
    j = pl.program_id(1)

    @pl.when(j == 0)
    def _():
        x = x_ref[...]
        ms = jnp.mean(x * x, axis=-1, keepdims=True)
        y = x * lax.rsqrt(ms + NORM_EPS) * g_ref[...]
        h_scr[...] = (y * (1.0 + mod_ref[1:2, :]) + mod_ref[0:1, :]).astype(BF16)

    h = h_scr[...]
    cos, sin_a, sin_b = cos_ref[...], sa_ref[...], sb_ref[...]
    c0 = 0
    r_q = jnp.dot(h, w_ref[:, c0:c0 + INPROJ_Q_COLS], preferred_element_type=F32)
    for hh in range(INPROJ_Q_COLS // HEAD_DIM):
        t = _head_norm_rope(r_q[:, hh * HEAD_DIM:(hh + 1) * HEAD_DIM], qg_ref[...], cos, sin_a, sin_b)
        q_ref[hh] = (t * Q_PRESCALE).astype(BF16)
    c0 += INPROJ_Q_COLS
    r_kv = jnp.dot(h, w_ref[:, c0:c0 + 2 * HEAD_DIM], preferred_element_type=F32)
    k_ref[0] = _head_norm_rope(r_kv[:, :HEAD_DIM], kg_ref[...], cos, sin_a, sin_b).astype(BF16)
    v_ref[0, 0:HEAD_DIM, :] = r_kv[:, HEAD_DIM:].T.astype(BF16)
    ones_row = lax.broadcasted_iota(jnp.int32, (VT_ROWS - HEAD_DIM, INPROJ_TM), 0) == 0
    v_ref[0, HEAD_DIM:VT_ROWS, :] = jnp.where(ones_row, 1.0, 0.0).astype(BF16)
    c0 += 2 * HEAD_DIM
    f_ref[0] = jnp.dot(h, w_ref[:, c0:c0 + FOURIER_GROUP_DIM], preferred_element_type=F32).astype(BF16)
    c0 += FOURIER_GROUP_DIM
    for s in range(INPROJ_GATE_COLS // INPROJ_GATE_SPLIT):
        r_g = jnp.dot(h, w_ref[:, c0:c0 + INPROJ_GATE_SPLIT], preferred_element_type=F32)
        gate_ref[:, s * INPROJ_GATE_SPLIT:(s + 1) * INPROJ_GATE_SPLIT] = jax.nn.sigmoid(r_g).astype(BF16)
        c0 += INPROJ_GATE_SPLIT


def _regroup_w_in(w_in):
    o1 = ATTN_Q_DIM
    o2 = o1 + ATTN_KV_DIM
    o3 = o2 + ATTN_KV_DIM
    o4 = o3 + FOURIER_DIM
    parts = []
    for j in range(INPROJ_GROUPS):
        parts += [
            w_in[:, j * INPROJ_Q_COLS:(j + 1) * INPROJ_Q_COLS],
            w_in[:, o1 + j * HEAD_DIM:o1 + (j + 1) * HEAD_DIM],
            w_in[:, o2 + j * HEAD_DIM:o2 + (j + 1) * HEAD_DIM],
            w_in[:, o3 + j * FOURIER_GROUP_DIM:o3 + (j + 1) * FOURIER_GROUP_DIM],
            w_in[:, o4 + j * INPROJ_GATE_COLS:o4 + (j + 1) * INPROJ_GATE_COLS],
        ]
    return jnp.concatenate(parts, axis=1).astype(BF16)


def _in_proj(x2, mod3, g_pre, w_grouped, cos, sin_a, sin_b, qg, kg):
    t_tokens = x2.shape[0]
    tm = INPROJ_TM
    tn = IN_PROJ_DIM // INPROJ_GROUPS
    tiles_per_seq = SEQ // tm
    row_tbl = pl.BlockSpec((tm, HEAD_DIM), lambda i, j: (i % tiles_per_seq, 0))
    vec_hd = pl.BlockSpec((1, HEAD_DIM), lambda i, j: (0, 0))
    return pl.pallas_call(
        _inproj_kernel,
        grid=(t_tokens // tm, INPROJ_GROUPS),
        in_specs=[
            pl.BlockSpec((tm, D_MODEL), lambda i, j: (i, 0)),
            pl.BlockSpec((None, 6, D_MODEL), lambda i, j: (i // tiles_per_seq, 0, 0)),
            pl.BlockSpec((1, D_MODEL), lambda i, j: (0, 0)),
            pl.BlockSpec((D_MODEL, tn), lambda i, j: (0, j)),
            row_tbl, row_tbl, row_tbl, vec_hd, vec_hd,
        ],
        out_specs=[
            pl.BlockSpec((INPROJ_Q_COLS // HEAD_DIM, tm, HEAD_DIM), lambda i, j: (j, i, 0)),
            pl.BlockSpec((1, tm, HEAD_DIM), lambda i, j: (j, i, 0)),
            pl.BlockSpec((1, VT_ROWS, tm), lambda i, j: (j, 0, i)),
            pl.BlockSpec((1, tm, FOURIER_GROUP_DIM), lambda i, j: (j, i, 0)),
            pl.BlockSpec((tm, INPROJ_GATE_COLS), lambda i, j: (i, j)),
        ],
        out_shape=[
            jax.ShapeDtypeStruct((N_HEADS, t_tokens, HEAD_DIM), BF16),
            jax.ShapeDtypeStruct((N_KV_HEADS, t_tokens, HEAD_DIM), BF16),
            jax.ShapeDtypeStruct((N_KV_HEADS, VT_ROWS, t_tokens), BF16),
            jax.ShapeDtypeStruct((N_FOURIER_GROUPS, t_tokens, FOURIER_GROUP_DIM), BF16),
            jax.ShapeDtypeStruct((t_tokens, 2 * D_MODEL), BF16),
        ],
        scratch_shapes=[pltpu.VMEM((tm, D_MODEL), BF16)],
        compiler_params=_cparams(("arbitrary", "arbitrary")),
        name="in_proj",
    )(x2, mod3, g_pre.reshape(1, D_MODEL), w_grouped, cos, sin_a, sin_b,
      qg.reshape(1, HEAD_DIM), kg.reshape(1, HEAD_DIM))


def _attn_kernel(q_ref, k_ref, vt_ref, o_ref, kmax_scr, m_scr, acc_scr):
    n_chunks = SEQ // ATTN_TK
    n_sub = ATTN_TQ // ATTN_SUB
    n_q = Q_PER_KV * ATTN_SUB
    qk_dims = (((1,), (1,)), ((), ()))

    def load_q(sub):
        return q_ref[:, sub * ATTN_SUB:(sub + 1) * ATTN_SUB, :].reshape(n_q, HEAD_DIM)

    def store_out(sub, acc):
        o_t = acc[:HEAD_DIM] * (1.0 / acc[HEAD_DIM:HEAD_DIM + 1])
        r0 = sub * ATTN_SUB
        for hh in range(Q_PER_KV):
            o_ref[r0:r0 + ATTN_SUB, hh * HEAD_DIM:(hh + 1) * HEAD_DIM] = (
                o_t[:, hh * ATTN_SUB:(hh + 1) * ATTN_SUB].T.astype(BF16))

    @pl.when(pl.program_id(2) == 0)
    def _():
        kf = k_ref[...].astype(F32)
        kn2 = jnp.sum(kf * kf, axis=1, keepdims=True)
        kmax_scr[...] = jnp.broadcast_to(jnp.max(kn2, axis=0, keepdims=True), kmax_scr.shape)

    qs = [load_q(sub) for sub in range(n_sub)]
    ones = jnp.ones((8, HEAD_DIM), BF16)
    shift = []
    for sub in range(n_sub):
        qf = qs[sub].astype(F32)
        qn2 = lax.dot_general(ones, (qf * qf).astype(BF16), qk_dims, preferred_element_type=F32)
        shift.append(jnp.sqrt(qn2[0:1] * kmax_scr[0:1, 0:1]))
    def scores(c, sub):
        k_c = k_ref[c * ATTN_TK:(c + 1) * ATTN_TK, :]
        return lax.dot_general(k_c, qs[sub], qk_dims, preferred_element_type=F32)

    units = [(c, sub) for c in range(n_chunks) for sub in range(n_sub)]
    acc = [None] * n_sub
    s_next = scores(*units[0])
    for u, (c, sub) in enumerate(units):
        s = s_next
        if u + 1 < len(units):
            s_next = scores(*units[u + 1])
        vt_c = vt_ref[:, c * ATTN_TK:(c + 1) * ATTN_TK]
        d = jnp.dot(vt_c, jnp.exp2(s - shift[sub]).astype(BF16), preferred_element_type=F32)
        acc[sub] = d if c == 0 else acc[sub] + d
    den_min = None
    for sub in range(n_sub):
        store_out(sub, acc[sub])
        den = acc[sub][HEAD_DIM:HEAD_DIM + 1]
        den_min = den if den_min is None else jnp.minimum(den_min, den)
    accurate = jnp.min(den_min) >= ATTN_MIN_DENOM

    @pl.when(jnp.logical_not(accurate))
    def _():
        for sub in range(n_sub):
            q = load_q(sub)
            m_scr[...] = jnp.full(m_scr.shape, -jnp.inf, F32)
            acc_scr[...] = jnp.zeros_like(acc_scr)

            def body(c, carry):
                off = pl.multiple_of(c * ATTN_TK, ATTN_TK)
                s = lax.dot_general(k_ref[pl.ds(off, ATTN_TK), :], q, qk_dims, preferred_element_type=F32)
                m_old = m_scr[...]
                m_new = jnp.maximum(m_old, jnp.max(s, axis=0, keepdims=True))
                p = jnp.exp2(s - m_new).astype(BF16)
                acc_scr[...] = (jnp.exp2(m_old - m_new) * acc_scr[...]
                                + jnp.dot(vt_ref[:, pl.ds(off, ATTN_TK)], p, preferred_element_type=F32))
                m_scr[...] = m_new
                return carry

            lax.fori_loop(0, n_chunks, body, 0)
            store_out(sub, acc_scr[...])


def _attention(q, k, vt):
    t_tokens = q.shape[1]
    n_b = t_tokens // SEQ
    qt = SEQ // ATTN_TQ
    return pl.pallas_call(
        _attn_kernel,
        grid=(n_b, N_KV_HEADS, qt),
        in_specs=[
            pl.BlockSpec((Q_PER_KV, ATTN_TQ, HEAD_DIM), lambda b, h, i: (h, b * qt + i, 0)),
            pl.BlockSpec((None, SEQ, HEAD_DIM), lambda b, h, i: (h, b, 0)),
            pl.BlockSpec((None, VT_ROWS, SEQ), lambda b, h, i: (h, 0, b)),
        ],
        out_specs=pl.BlockSpec((ATTN_TQ, Q_PER_KV * HEAD_DIM), lambda b, h, i: (b * qt + i, h)),
        out_shape=jax.ShapeDtypeStruct((t_tokens, ATTN_Q_DIM), BF16),
        scratch_shapes=[
            pltpu.VMEM((8, LANES), F32),
            pltpu.VMEM((1, Q_PER_KV * ATTN_SUB), F32),
            pltpu.VMEM((VT_ROWS, Q_PER_KV * ATTN_SUB), F32),
        ],
        compiler_params=_cparams(("arbitrary", "arbitrary", "arbitrary")),
        name="attention",
    )(q, k, vt)


def _fourier_tables():
    r = FFT_RADIX
    n = FOURIER_GROUP_DIM
    ch = np.arange(n)
    ang_c = 2.0 * np.pi * np.outer(ch, ch) / n
    scale = 1.0 / math.sqrt(SEQ * n)
    w_ch = np.concatenate([np.cos(ang_c), -np.sin(ang_c)], axis=1) * scale
    a = np.arange(r)
    ang1 = 2.0 * np.pi * np.outer(a, a) / r
    c1, s1 = np.cos(ang1), np.sin(ang1)
    w1 = np.zeros((r, 2, 2, r))
    w1[:, 0, 0, :] = c1
    w1[:, 0, 1, :] = s1
    w1[:, 1, 0, :] = -s1
    w1[:, 1, 1, :] = c1
    w1 = w1.reshape(2 * r, 2 * r)
    sp = (np.arange(r)[:, None] + r * np.arange(r)[None, :])
    th = 2.0 * np.pi * sp[:, :, None] * np.arange(r)[None, None, :] / SEQ
    w2 = np.concatenate([np.cos(th), np.sin(th)], axis=2)
    return tuple(jnp.asarray(t, dtype=F32).astype(BF16) for t in (w_ch, w1, w2))


def _fourier_kernel(f_ref, wch_ref, w1_ref, w2_ref, o_ref, x_scr, y_scr, o_scr):
    r = FFT_RADIX
    n = FOURIER_GROUP_DIM
    z = jnp.dot(f_ref[...], wch_ref[...], preferred_element_type=F32)
    x_scr[0:r] = z[:, :n].astype(BF16).reshape(r, r, n)
    x_scr[r:2 * r] = z[:, n:].astype(BF16).reshape(r, r, n)
    y = lax.dot_general(w1_ref[...], x_scr[...], (((1,), (0,)), ((), ())),
                        preferred_element_type=F32)
    y_scr[...] = y.astype(BF16)

    def body(c, carry):
        slab = y_scr[pl.ds(2 * c, 2)].reshape(2 * r, n)
        res = jnp.dot(w2_ref[c], slab, preferred_element_type=F32)
        for s in range(n // LANES):
            o_scr[s, pl.ds(c, r, stride=r), :] = res[:, s * LANES:(s + 1) * LANES]
        return carry

    lax.fori_loop(0, r, body, 0, unroll=FFT_STAGE2_UNROLL)
    for s in range(n // LANES):
        o_ref[:, s * LANES:(s + 1) * LANES] = o_scr[s].astype(BF16)


def _fourier(f):
    t_tokens = f.shape[1]
    n_b = t_tokens // SEQ
    r = FFT_RADIX
    n = FOURIER_GROUP_DIM
    w_ch, w1, w2 = _fourier_tables()
    blk = pl.BlockSpec((None, SEQ, n), lambda g, b: (g, b, 0))
    return pl.pallas_call(
        _fourier_kernel,
        grid=(N_FOURIER_GROUPS, n_b),
        in_specs=[
            blk,
            pl.BlockSpec((n, 2 * n), lambda g, b: (0, 0)),
            pl.BlockSpec((2 * r, 2 * r), lambda g, b: (0, 0)),
            pl.BlockSpec((r, r, 2 * r), lambda g, b: (0, 0, 0)),
        ],
        out_specs=blk,
        out_shape=jax.ShapeDtypeStruct((N_FOURIER_GROUPS, t_tokens, n), BF16),
        scratch_shapes=[
            pltpu.VMEM((2 * r, r, n), BF16),
            pltpu.VMEM((2 * r, r, n), BF16),
            pltpu.VMEM((n // LANES, SEQ, LANES), F32),
        ],
        compiler_params=_cparams(("arbitrary", "arbitrary")),
        name="fourier",
    )(f, w_ch, w1, w2)


def _post_mix_kernel(attn_ref, four_ref, gate_ref, x_ref, mod_ref, gpost_ref, gpre_ref,
                     wao_ref, wfo_ref, wmo_ref, wr_ref, br_ref,
                     x1_ref, h2_ref, logit_ref):
    a = jnp.dot(attn_ref[...], wao_ref[...], preferred_element_type=F32)
    fo = jnp.dot(four_ref[0], wfo_ref[0], preferred_element_type=F32)
    for g in range(1, N_FOURIER_GROUPS):
        fo = fo + jnp.dot(four_ref[g], wfo_ref[g], preferred_element_type=F32)
    ga = gate_ref[:, :D_MODEL].astype(F32)
    gf = gate_ref[:, D_MODEL:].astype(F32)
    m = (ga * a + gf * fo).astype(BF16)
    mixed = jnp.dot(m, wmo_ref[...], preferred_element_type=F32)
    ms = jnp.mean(mixed * mixed, axis=-1, keepdims=True)
    y = mixed * lax.rsqrt(ms + NORM_EPS) * gpost_ref[...]
    x1 = x_ref[...] + mod_ref[2:3, :] * y
    x1_ref[...] = x1
    ms1 = jnp.mean(x1 * x1, axis=-1, keepdims=True)
    h2 = (x1 * lax.rsqrt(ms1 + NORM_EPS) * gpre_ref[...]) * (1.0 + mod_ref[4:5, :]) + mod_ref[3:4, :]
    h2_ref[...] = h2
    logit_ref[...] = jnp.dot(h2.astype(BF16), wr_ref[...], preferred_element_type=F32) + br_ref[...]


def _post_mix(attn, four, gates, x2, mod3, g_post, g_pre, wao, wfo4, wmo, w_router, b_router):
    t_tokens = x2.shape[0]
    tm = POST_TM
    tiles_per_seq = SEQ // tm
    n = FOURIER_GROUP_DIM
    const2 = lambda i: (0, 0)
    resident = dict(pipeline_mode=pl.Buffered(1))
    return pl.pallas_call(
        _post_mix_kernel,
        grid=(t_tokens // tm,),
        in_specs=[
            pl.BlockSpec((tm, ATTN_Q_DIM), lambda i: (i, 0)),
            pl.BlockSpec((N_FOURIER_GROUPS, tm, n), lambda i: (0, i, 0)),
            pl.BlockSpec((tm, 2 * D_MODEL), lambda i: (i, 0)),
            pl.BlockSpec((tm, D_MODEL), lambda i: (i, 0)),
            pl.BlockSpec((None, 6, D_MODEL), lambda i: (i // tiles_per_seq, 0, 0)),
            pl.BlockSpec((1, D_MODEL), const2),
            pl.BlockSpec((1, D_MODEL), const2),
            pl.BlockSpec((ATTN_Q_DIM, D_MODEL), const2, **resident),
            pl.BlockSpec((N_FOURIER_GROUPS, n, D_MODEL), lambda i: (0, 0, 0), **resident),
            pl.BlockSpec((D_MODEL, D_MODEL), const2, **resident),
            pl.BlockSpec((D_MODEL, LANES), const2, **resident),
            pl.BlockSpec((1, LANES), const2),
        ],
        out_specs=[
            pl.BlockSpec((tm, D_MODEL), lambda i: (i, 0)),
            pl.BlockSpec((tm, D_MODEL), lambda i: (i, 0)),
            pl.BlockSpec((tm, LANES), lambda i: (i, 0)),
        ],
        out_shape=[
            jax.ShapeDtypeStruct((t_tokens, D_MODEL), F32),
            jax.ShapeDtypeStruct((t_tokens, D_MODEL), F32),
            jax.ShapeDtypeStruct((t_tokens, LANES), F32),
        ],
        compiler_params=_cparams(("arbitrary",)),
        name="post_mix",
    )(attn, four, gates, x2, mod3, g_post.reshape(1, D_MODEL), g_pre.reshape(1, D_MODEL),
      wao, wfo4, wmo, w_router, b_router)


def _first_argmax(vals, lane):
    mx = jnp.max(vals, axis=-1, keepdims=True)
    idx = jnp.min(jnp.where(vals == mx, lane, float(LANES)), axis=-1, keepdims=True)
    return mx, idx


def _route_kernel(logit_ref, tri_ref, out_ref, cnt_ref, carry_scr, start_scr):
    phase = pl.program_id(0)
    i = pl.program_id(1)

    @pl.when((phase == 0) & (i == 0))
    def _():
        carry_scr[...] = jnp.zeros_like(carry_scr)
        start_scr[...] = jnp.zeros_like(start_scr)

    @pl.when((phase == 1) & (i == 0))
    def _():
        counts = carry_scr[...]
        lane8 = lax.broadcasted_iota(jnp.int32, counts.shape, 1)
        incl = counts
        shift = 1
        while shift < N_EXPERTS:
            incl = incl + jnp.where(lane8 >= N_GROUPS + shift, pltpu.roll(incl, shift, 1), 0.0)
            shift *= 2
        start_scr[...] = incl - counts
        carry_scr[...] = jnp.zeros_like(carry_scr)

    lg = logit_ref[...]
    lane = lax.broadcasted_iota(jnp.int32, lg.shape, 1).astype(F32)
    neg = jnp.float32(-jnp.inf)
    gl = jnp.where(lane < N_GROUPS, lg, neg)
    gmax, gidx = _first_argmax(gl, lane)
    g_w = 1.0 / jnp.sum(jnp.exp(gl - gmax), axis=-1, keepdims=True)
    lo = N_GROUPS + EXPERTS_PER_GROUP * gidx
    el = jnp.where((lane >= lo) & (lane < lo + EXPERTS_PER_GROUP), lg, neg)
    m1, i1 = _first_argmax(el, lane)
    el2 = jnp.where(lane == i1, neg, el)
    m2, i2 = _first_argmax(el2, lane)
    p2 = jnp.exp(m2 - m1)
    w1 = g_w / (1.0 + p2)
    w2 = g_w * p2 / (1.0 + p2)
    oh = jnp.where((lane == i1) | (lane == i2), 1.0, 0.0)
    before = (jnp.dot(tri_ref[...], oh.astype(BF16), preferred_element_type=F32)
              + carry_scr[0:1, :] + start_scr[0:1, :])
    pos1 = jnp.sum(jnp.where(lane == i1, before, 0.0), axis=-1, keepdims=True)
    pos2 = jnp.sum(jnp.where(lane == i2, before, 0.0), axis=-1, keepdims=True)
    carry_scr[...] = carry_scr[...] + jnp.sum(oh, axis=0, keepdims=True)
    out = jnp.where(lane == 0, i1 - N_GROUPS, 0.0)
    out = jnp.where(lane == 1, i2 - N_GROUPS, out)
    out = jnp.where(lane == 2, pos1, out)
    out = jnp.where(lane == 3, pos2, out)
    out = jnp.where(lane == 4, w1, out)
    out = jnp.where(lane == 5, w2, out)
    out_ref[...] = out
    cnt_ref[...] = carry_scr[...]


def _route(logits):
    t_tokens = logits.shape[0]
    tb = ROUTE_TB
    tri = jnp.asarray(np.tril(np.ones((tb, tb), np.float32), -1), dtype=BF16)
    return pl.pallas_call(
        _route_kernel,
        grid=(2, t_tokens // tb),
        in_specs=[
            pl.BlockSpec((tb, LANES), lambda p, i: (i, 0)),
            pl.BlockSpec((tb, tb), lambda p, i: (0, 0)),
        ],
        out_specs=[
            pl.BlockSpec((tb, LANES), lambda p, i: (i * p, 0)),
            pl.BlockSpec((8, LANES), lambda p, i: (0, 0)),
        ],
        out_shape=[
            jax.ShapeDtypeStruct((t_tokens, LANES), F32),
            jax.ShapeDtypeStruct((8, LANES), F32),
        ],
        scratch_shapes=[pltpu.VMEM((8, LANES), F32), pltpu.VMEM((8, LANES), F32)],
        compiler_params=_cparams(("arbitrary", "arbitrary")),
        name="route",
    )(logits, tri)


def _row_gather(src_hbm, src_row, j, dst, sem):
    return pltpu.make_async_copy(src_hbm.at[pl.ds(src_row, 1), :], dst.at[pl.ds(j, 1), :], sem)


def _start_row_gathers(src_hbm, idx_ref, dst, sem, n_rows):
    def body(jj, carry):
        for u in range(GATHER_UNROLL):
            j = jj * GATHER_UNROLL + u
            _row_gather(src_hbm, idx_ref[0, 0, j], j, dst, sem).start(priority=u % 2)
        return carry

    lax.fori_loop(0, n_rows // GATHER_UNROLL, body, 0)


def _wait_row_gathers(src_hbm, dst, sem, n_rows):
    def body(jj, carry):
        for u in range(GATHER_UNROLL):
            _row_gather(src_hbm, 0, jj * GATHER_UNROLL + u, dst, sem).wait()
        return carry

    lax.fori_loop(0, n_rows // GATHER_UNROLL, body, 0)


def _row_scatter(src, j, dst_hbm, dst_row, sem):
    return pltpu.make_async_copy(src.at[pl.ds(j, 1), :], dst_hbm.at[pl.ds(dst_row, 1), :], sem)


def _dispatch_kernel(p1_ref, p2_ref, h_ref, xs_hbm, buf, sem):
    i = pl.program_id(0)
    slot = i % 2
    src = buf.at[slot]
    src[...] = h_ref[...]

    def start(jj, carry):
        for u in range(GATHER_UNROLL):
            j = jj * GATHER_UNROLL + u
            _row_scatter(src, j, xs_hbm, p1_ref[0, 0, j], sem.at[slot]).start(priority=0)
            _row_scatter(src, j, xs_hbm, p2_ref[0, 0, j], sem.at[slot]).start(priority=1)
        return carry

    lax.fori_loop(0, MOE_TB // GATHER_UNROLL, start, 0)

    def wait_block(s):
        def body(jj, carry):
            for u in range(GATHER_UNROLL):
                j = jj * GATHER_UNROLL + u
                _row_scatter(buf.at[s], j, xs_hbm, 0, sem.at[s]).wait()
                _row_scatter(buf.at[s], j, xs_hbm, 0, sem.at[s]).wait()
            return carry

        lax.fori_loop(0, MOE_TB // GATHER_UNROLL, body, 0)

    @pl.when(i > 0)
    def _():
        wait_block(1 - slot)

    @pl.when(i + 1 == pl.num_programs(0))
    def _():
        wait_block(slot)


def _dispatch(pos1, pos2, h2):
    t_tokens = h2.shape[0]
    nb = t_tokens // MOE_TB
    smem_blk = pl.BlockSpec((1, 1, MOE_TB), lambda i: (i, 0, 0), memory_space=pltpu.SMEM)
    return pl.pallas_call(
        _dispatch_kernel,
        grid=(nb,),
        in_specs=[smem_blk, smem_blk, pl.BlockSpec((MOE_TB, D_MODEL), lambda i: (i, 0))],
        out_specs=pl.BlockSpec(memory_space=pl.ANY),
        out_shape=jax.ShapeDtypeStruct((2 * t_tokens, D_MODEL), F32),
        scratch_shapes=[pltpu.VMEM((2, MOE_TB, D_MODEL), F32), pltpu.SemaphoreType.DMA((2,))],
        compiler_params=_cparams(("arbitrary",)),
        name="dispatch",
    )(pos1.reshape(nb, 1, MOE_TB), pos2.reshape(nb, 1, MOE_TB), h2)


def _expert_kernel(tile_ref, exp_ref, start_ref, end_ref, nitem_ref,
                   xs_ref, wg_ref, wu_ref, wd_ref, ys_ref, wg_scr, wu_scr, wd_scr):
    k = pl.program_id(0)
    tile = tile_ref[k]
    e = exp_ref[k]
    prev = jnp.maximum(k - 1, 0)
    first_visit = (k == 0) | (tile != tile_ref[prev])
    new_expert = (k == 0) | (e != exp_ref[prev])

    @pl.when(new_expert)
    def _():
        wg_scr[...] = wg_ref[...].astype(BF16)
        wu_scr[...] = wu_ref[...].astype(BF16)
        wd_scr[...] = wd_ref[...].astype(BF16)

    @pl.when(first_visit)
    def _():
        ys_ref[...] = jnp.zeros_like(ys_ref)

    @pl.when(k < nitem_ref[0])
    def _():
        x = xs_ref[...].astype(BF16)
        g = jnp.dot(x, wg_scr[...], preferred_element_type=F32)
        u = jnp.dot(x, wu_scr[...], preferred_element_type=F32)
        hmid = (g * jax.nn.sigmoid(g) * u).astype(BF16)
        y = jnp.dot(hmid, wd_scr[...], preferred_element_type=F32)
        row = tile * MOE_TM + lax.broadcasted_iota(jnp.int32, (MOE_TM, 1), 0)
        valid = (row >= start_ref[e]) & (row < end_ref[e])
        ys_ref[...] += jnp.where(valid, y, 0.0)


def _experts(item_tile, item_exp, starts, ends, n_items, xs, wg, wu, wd):
    n_rows = xs.shape[0]
    n_tiles = n_rows // MOE_TM
    max_items = n_tiles + N_EXPERTS - 1
    grid_spec = pltpu.PrefetchScalarGridSpec(
        num_scalar_prefetch=5,
        grid=(max_items,),
        in_specs=[
            pl.BlockSpec((MOE_TM, D_MODEL), lambda k, t, e, s, en, n: (t[k], 0)),
            pl.BlockSpec((None, D_MODEL, EXPERT_FF), lambda k, t, e, s, en, n: (e[k], 0, 0)),
            pl.BlockSpec((None, D_MODEL, EXPERT_FF), lambda k, t, e, s, en, n: (e[k], 0, 0)),
            pl.BlockSpec((None, EXPERT_FF, D_MODEL), lambda k, t, e, s, en, n: (e[k], 0, 0)),
        ],
        out_specs=pl.BlockSpec((MOE_TM, D_MODEL), lambda k, t, e, s, en, n: (t[k], 0)),
        scratch_shapes=[
            pltpu.VMEM((D_MODEL, EXPERT_FF), BF16),
            pltpu.VMEM((D_MODEL, EXPERT_FF), BF16),
            pltpu.VMEM((EXPERT_FF, D_MODEL), BF16),
        ],
    )
    return pl.pallas_call(
        _expert_kernel,
        grid_spec=grid_spec,
        out_shape=jax.ShapeDtypeStruct((n_rows, D_MODEL), F32),
        compiler_params=_cparams(("arbitrary",)),
        name="experts",
    )(item_tile, item_exp, starts, ends, n_items, xs, wg, wu, wd)


def _expert_items(counts):
    ends = jnp.cumsum(counts)
    starts = ends - counts
    first_tile = starts // MOE_TM
    last_tile = (ends - 1) // MOE_TM
    n_it = jnp.where(counts > 0, last_tile - first_tile + 1, 0)
    it_end = jnp.cumsum(n_it)
    it_start = it_end - n_it
    total = it_end[-1]
    return starts, ends, first_tile, it_start, it_end, total


def _combine_kernel(p1_ref, p2_ref, p1_next_ref, p2_next_ref, ys_hbm, route_ref, x1_ref, mod_ref, g_ref,
                    o_ref, y_scr, sem):
    i = pl.program_id(0)
    slot = i % 2

    def start_block(pa_ref, pb_ref, s):
        _start_row_gathers(ys_hbm, pa_ref, y_scr.at[s, 0], sem.at[s], MOE_TB)
        _start_row_gathers(ys_hbm, pb_ref, y_scr.at[s, 1], sem.at[s], MOE_TB)

    @pl.when(i == 0)
    def _():
        start_block(p1_ref, p2_ref, 0)

    @pl.when(i + 1 < pl.num_programs(0))
    def _():
        start_block(p1_next_ref, p2_next_ref, 1 - slot)

    _wait_row_gathers(ys_hbm, y_scr.at[slot, 0], sem.at[slot], MOE_TB)
    _wait_row_gathers(ys_hbm, y_scr.at[slot, 1], sem.at[slot], MOE_TB)
    w1 = route_ref[:, 4:5]
    w2 = route_ref[:, 5:6]
    ffn = w1 * y_scr[slot, 0] + w2 * y_scr[slot, 1]
    ms = jnp.mean(ffn * ffn, axis=-1, keepdims=True)
    y = ffn * lax.rsqrt(ms + NORM_EPS) * g_ref[...]
    o_ref[...] = x1_ref[...] + mod_ref[5:6, :] * y


def _combine(pos1, pos2, ys, route, x1, mod3, g_post):
    t_tokens = x1.shape[0]
    tb = MOE_TB
    nb = t_tokens // tb
    tiles_per_seq = SEQ // tb
    smem_blk = pl.BlockSpec((1, 1, tb), lambda i: (i, 0, 0), memory_space=pltpu.SMEM)
    smem_next = pl.BlockSpec((1, 1, tb), lambda i: (jnp.minimum(i + 1, nb - 1), 0, 0), memory_space=pltpu.SMEM)
    p1 = pos1.reshape(nb, 1, tb)
    p2 = pos2.reshape(nb, 1, tb)
    return pl.pallas_call(
        _combine_kernel,
        grid=(nb,),
        in_specs=[
            smem_blk, smem_blk, smem_next, smem_next,
            pl.BlockSpec(memory_space=pl.ANY),
            pl.BlockSpec((tb, LANES), lambda i: (i, 0)),
            pl.BlockSpec((tb, D_MODEL), lambda i: (i, 0)),
            pl.BlockSpec((None, 6, D_MODEL), lambda i: (i // tiles_per_seq, 0, 0)),
            pl.BlockSpec((1, D_MODEL), lambda i: (0, 0)),
        ],
        out_specs=pl.BlockSpec((tb, D_MODEL), lambda i: (i, 0)),
        out_shape=jax.ShapeDtypeStruct((t_tokens, D_MODEL), F32),
        scratch_shapes=[
            pltpu.VMEM((2, 2, tb, D_MODEL), F32),
            pltpu.SemaphoreType.DMA((2,)),
        ],
        compiler_params=_cparams(("arbitrary",)),
        name="combine",
    )(p1, p2, p1, p2, ys, route, x1, mod3, g_post.reshape(1, D_MODEL))


def _rope_tables():
    rows = SEQ // GRID_W
    row = jnp.repeat(jnp.arange(rows, dtype=F32), GRID_W)
    col = jnp.tile(jnp.arange(GRID_W, dtype=F32), rows)
    n_freq = HEAD_DIM // 4
    inv = ROPE_THETA ** (-jnp.arange(n_freq, dtype=F32) / n_freq)
    ang_r = row[:, None] * inv
    ang_c = col[:, None] * inv
    ang = jnp.concatenate([ang_r, ang_r, ang_c, ang_c], axis=-1)
    cos, sin = jnp.cos(ang), jnp.sin(ang)
    first = (jnp.arange(HEAD_DIM) % (2 * n_freq)) < n_freq
    sin_a = jnp.where(first, -sin, 0.0)
    sin_b = jnp.where(first, 0.0, sin)
    return cos, sin_a, sin_b


def kernel(x, c, w_ada, b_ada, g_pre_mix, g_post_mix, w_in, q_norm_g, k_norm_g, w_attn_out, w_fourier_out,
           w_mix_out, g_pre_ffn, g_post_ffn, w_group_router, b_group_router, w_expert_router, b_expert_router,
           w_exp_gate, w_exp_up, w_exp_down):
    n_b, seq, d = x.shape
    assert seq == SEQ and d == D_MODEL
    t_tokens = n_b * seq
    x2 = x.reshape(t_tokens, d)

    c_pad = jnp.zeros((8, d), F32).at[:n_b].set(c)
    mod = _ada(c_pad, w_ada, b_ada)[:n_b]
    mod3 = mod.reshape(n_b, 6, d)

    cos, sin_a, sin_b = _rope_tables()
    q, k, vt, f, gates = _in_proj(x2, mod3, g_pre_mix, _regroup_w_in(w_in), cos, sin_a, sin_b, q_norm_g, k_norm_g)
    attn = _attention(q, k, vt)
    four = _fourier(f)

    n_r = N_GROUPS + N_EXPERTS
    w_router = jnp.zeros((d, LANES), F32).at[:, :N_GROUPS].set(w_group_router).at[:, N_GROUPS:n_r].set(w_expert_router)
    b_router = jnp.zeros((1, LANES), F32).at[0, :N_GROUPS].set(b_group_router).at[0, N_GROUPS:n_r].set(b_expert_router)
    x1, h2, logits = _post_mix(
        attn, four, gates, x2, mod3, g_post_mix, g_pre_ffn,
        w_attn_out.astype(BF16),
        w_fourier_out.astype(BF16).reshape(N_FOURIER_GROUPS, FOURIER_GROUP_DIM, d),
        w_mix_out.astype(BF16), w_router.astype(BF16), b_router)

    route, cnt = _route(logits)
    counts = cnt[0, N_GROUPS:n_r].astype(jnp.int32)
    starts, ends, first_tile, it_start, it_end, total = _expert_items(counts)
    pos1 = route[:, 2].astype(jnp.int32)
    pos2 = route[:, 3].astype(jnp.int32)

    xs = _dispatch(pos1, pos2, h2)

    n_tiles = 2 * t_tokens // MOE_TM
    max_items = n_tiles + N_EXPERTS - 1
    kk = jnp.minimum(jnp.arange(max_items, dtype=jnp.int32), total - 1)
    item_exp = jnp.sum((it_end[None, :] <= kk[:, None]).astype(jnp.int32), axis=1)
    item_tile = first_tile[item_exp] + kk - it_start[item_exp]
    ys = _experts(item_tile.astype(jnp.int32), item_exp.astype(jnp.int32), starts.astype(jnp.int32),
                  ends.astype(jnp.int32), total.reshape(1).astype(jnp.int32), xs, w_exp_gate, w_exp_up, w_exp_down)

    out = _combine(pos1, pos2, ys, route, x1, mod3, g_post_ffn)
    return out.reshape(n_b, seq, d)
```

```python
import math

import numpy as np
import jax
import jax.numpy as jnp
from jax import lax
from jax.experimental import pallas as pl
from jax.experimental.pallas import tpu as pltpu

F32 = jnp.float32
BF16 = jnp.bfloat16

D_MODEL = 2048
SEQ = 4096
N_HEADS = 16
N_KV_HEADS = 4
Q_PER_KV = N_HEADS // N_KV_HEADS
HEAD_DIM = 128
ROPE_THETA = 10000.0
GRID_W = 64
ATTN_Q_DIM = N_HEADS * HEAD_DIM
ATTN_KV_DIM = N_KV_HEADS * HEAD_DIM
N_FOURIER_GROUPS = 4
FOURIER_GROUP_DIM = 256
FOURIER_DIM = N_FOURIER_GROUPS * FOURIER_GROUP_DIM
IN_PROJ_DIM = ATTN_Q_DIM + 2 * ATTN_KV_DIM + FOURIER_DIM + 2 * D_MODEL
N_GROUPS = 4
EXPERTS_PER_GROUP = 8
N_EXPERTS = N_GROUPS * EXPERTS_PER_GROUP
EXPERT_FF = 512
NORM_EPS = 1e-6

LANES = 128
FFT_RADIX = 64
FFT_STAGE2_UNROLL = 8
VMEM_LIMIT = 56 * 1024 * 1024

Q_PRESCALE = HEAD_DIM ** -0.5 * math.log2(math.e)

ADA_TN = 1024
INPROJ_TM = 512
INPROJ_GROUPS = N_KV_HEADS
INPROJ_Q_COLS = ATTN_Q_DIM // INPROJ_GROUPS
INPROJ_GATE_COLS = 2 * D_MODEL // INPROJ_GROUPS
INPROJ_GATE_SPLIT = 512
VT_ROWS = HEAD_DIM + 16
ATTN_TQ = 1024
ATTN_SUB = 128
ATTN_TK = 512
ATTN_MIN_DENOM = 2.0 ** -64
POST_TM = 256
ROUTE_TB = 1024
MOE_TB = 256
MOE_TM = 512
GATHER_UNROLL = 8


def _cparams(sem, vmem=VMEM_LIMIT):
    return pltpu.CompilerParams(dimension_semantics=sem, vmem_limit_bytes=vmem)


def _ada_kernel(c_ref, w_ref, b_ref, o_ref):
    cs = c_ref[...]
    s = cs * jax.nn.sigmoid(cs)
    o_ref[...] = jnp.dot(s.astype(BF16), w_ref[...].astype(BF16), preferred_element_type=F32) + b_ref[...]


def _ada(c_pad, w_ada, b_ada):
    n = w_ada.shape[1]
    return pl.pallas_call(
        _ada_kernel,
        grid=(n // ADA_TN,),
        in_specs=[
            pl.BlockSpec((8, D_MODEL), lambda j: (0, 0)),
            pl.BlockSpec((D_MODEL, ADA_TN), lambda j: (0, j)),
            pl.BlockSpec((1, ADA_TN), lambda j: (0, j)),
        ],
        out_specs=pl.BlockSpec((8, ADA_TN), lambda j: (0, j)),
        out_shape=jax.ShapeDtypeStruct((8, n), F32),
        compiler_params=_cparams(("arbitrary",)),
        name="ada",
    )(c_pad, w_ada, b_ada.reshape(1, n))


def _head_norm_rope(t, g, cos, sin_a, sin_b):
    ms = jnp.mean(t * t, axis=-1, keepdims=True)
    y = t * lax.rsqrt(ms + NORM_EPS) * g
    return y * cos + pltpu.roll(y, 96, 1) * sin_a + pltpu.roll(y, 32, 1) * sin_b


def _inproj_kernel(x_ref, mod_ref, g_ref, w_ref, cos_ref, sa_ref, sb_ref, qg_ref, kg_ref,
                   q_ref, k_ref, v_ref, f_ref, gate_ref, h_scr):
    j = pl.program_id(1)

    @pl.when(j == 0)
    def _():
        x = x_ref[...]
        ms = jnp.mean(x * x, axis=-1, keepdims=True)
        y = x * lax.rsqrt(ms + NORM_EPS) * g_ref[...]
        h_scr[...] = (y * (1.0 + mod_ref[1:2, :]) + mod_ref[0:1, :]).astype(BF16)

    h = h_scr[...]
    cos, sin_a, sin_b = cos_ref[...], sa_ref[...], sb_ref[...]
    c0 = 0
    r_q = jnp.dot(h, w_ref[:, c0:c0 + INPROJ_Q_COLS], preferred_element_type=F32)
    for hh in range(INPROJ_Q_COLS // HEAD_DIM):
        t = _head_norm_rope(r_q[:, hh * HEAD_DIM:(hh + 1) * HEAD_DIM], qg_ref[...], cos, sin_a, sin_b)
        q_ref[hh] = (t * Q_PRESCALE).astype(BF16)
    c0 += INPROJ_Q_COLS
    r_kv = jnp.dot(h, w_ref[:, c0:c0 + 2 * HEAD_DIM], preferred_element_type=F32)
    k_ref[0] = _head_norm_rope(r_kv[:, :HEAD_DIM], kg_ref[...], cos, sin_a, sin_b).astype(BF16)
    v_ref[0, 0:HEAD_DIM, :] = r_kv[:, HEAD_DIM:].T.astype(BF16)
    ones_row = lax.broadcasted_iota(jnp.int32, (VT_ROWS - HEAD_DIM, INPROJ_TM), 0) == 0
    v_ref[0, HEAD_DIM:VT_ROWS, :] = jnp.where(ones_row, 1.0, 0.0).astype(BF16)
    c0 += 2 * HEAD_DIM
    f_ref[0] = jnp.dot(h, w_ref[:, c0:c0 + FOURIER_GROUP_DIM], preferred_element_type=F32).astype(BF16)
    c0 += FOURIER_GROUP_DIM
    for s in range(INPROJ_GATE_COLS // INPROJ_GATE_SPLIT):
        r_g = jnp.dot(h, w_ref[:, c0:c0 + INPROJ_GATE_SPLIT], preferred_element_type=F32)
        gate_ref[:, s * INPROJ_GATE_SPLIT:(s + 1) * INPROJ_GATE_SPLIT] = jax.nn.sigmoid(r_g).astype(BF16)
        c0 += INPROJ_GATE_SPLIT


def _regroup_w_in(w_in):
    o1 = ATTN_Q_DIM
    o2 = o1 + ATTN_KV_DIM
    o3 = o2 + ATTN_KV_DIM
    o4 = o3 + FOURIER_DIM
    parts = []
    for j in range(INPROJ_GROUPS):
        parts += [
            w_in[:, j * INPROJ_Q_COLS:(j + 1) * INPROJ_Q_COLS],
            w_in[:, o1 + j * HEAD_DIM:o1 + (j + 1) * HEAD_DIM],
            w_in[:, o2 + j * HEAD_DIM:o2 + (j + 1) * HEAD_DIM],
            w_in[:, o3 + j * FOURIER_GROUP_DIM:o3 + (j + 1) * FOURIER_GROUP_DIM],
            w_in[:, o4 + j * INPROJ_GATE_COLS:o4 + (j + 1) * INPROJ_GATE_COLS],
        ]
    return jnp.concatenate(parts, axis=1).astype(BF16)


def _in_proj(x2, mod3, g_pre, w_grouped, cos, sin_a, sin_b, qg, kg):
    t_tokens = x2.shape[0]
    tm = INPROJ_TM
    tn = IN_PROJ_DIM // INPROJ_GROUPS
    tiles_per_seq = SEQ // tm
    row_tbl = pl.BlockSpec((tm, HEAD_DIM), lambda i, j: (i % tiles_per_seq, 0))
    vec_hd = pl.BlockSpec((1, HEAD_DIM), lambda i, j: (0, 0))
    return pl.pallas_call(
        _inproj_kernel,
        grid=(t_tokens // tm, INPROJ_GROUPS),
        in_specs=[
            pl.BlockSpec((tm, D_MODEL), lambda i, j: (i, 0)),
            pl.BlockSpec((None, 6, D_MODEL), lambda i, j: (i // tiles_per_seq, 0, 0)),
            pl.BlockSpec((1, D_MODEL), lambda i, j: (0, 0)),
            pl.BlockSpec((D_MODEL, tn), lambda i, j: (0, j)),
            row_tbl, row_tbl, row_tbl, vec_hd, vec_hd,
        ],
        out_specs=[
            pl.BlockSpec((INPROJ_Q_COLS // HEAD_DIM, tm, HEAD_DIM), lambda i, j: (j, i, 0)),
            pl.BlockSpec((1, tm, HEAD_DIM), lambda i, j: (j, i, 0)),
            pl.BlockSpec((1, VT_ROWS, tm), lambda i, j: (j, 0, i)),
            pl.BlockSpec((1, tm, FOURIER_GROUP_DIM), lambda i, j: (j, i, 0)),
            pl.BlockSpec((tm, INPROJ_GATE_COLS), lambda i, j: (i, j)),
        ],
        out_shape=[
            jax.ShapeDtypeStruct((N_HEADS, t_tokens, HEAD_DIM), BF16),
            jax.ShapeDtypeStruct((N_KV_HEADS, t_tokens, HEAD_DIM), BF16),
            jax.ShapeDtypeStruct((N_KV_HEADS, VT_ROWS, t_tokens), BF16),
            jax.ShapeDtypeStruct((N_FOURIER_GROUPS, t_tokens, FOURIER_GROUP_DIM), BF16),
            jax.ShapeDtypeStruct((t_tokens, 2 * D_MODEL), BF16),
        ],
        scratch_shapes=[pltpu.VMEM((tm, D_MODEL), BF16)],
        compiler_params=_cparams(("arbitrary", "arbitrary")),
        name="in_proj",
    )(x2, mod3, g_pre.reshape(1, D_MODEL), w_grouped, cos, sin_a, sin_b,
      qg.reshape(1, HEAD_DIM), kg.reshape(1, HEAD_DIM))


def _attn_kernel(q_ref, k_ref, vt_ref, o_ref, kmax_scr, m_scr, acc_scr):
    n_chunks = SEQ // ATTN_TK
    n_sub = ATTN_TQ // ATTN_SUB
    n_q = Q_PER_KV * ATTN_SUB
    qk_dims = (((1,), (1,)), ((), ()))

    def load_q(sub):
        return q_ref[:, sub * ATTN_SUB:(sub + 1) * ATTN_SUB, :].reshape(n_q, HEAD_DIM)

    def store_out(sub, acc):
        o_t = acc[:HEAD_DIM] * (1.0 / acc[HEAD_DIM:HEAD_DIM + 1])
        r0 = sub * ATTN_SUB
        for hh in range(Q_PER_KV):
            o_ref[r0:r0 + ATTN_SUB, hh * HEAD_DIM:(hh + 1) * HEAD_DIM] = (
                o_t[:, hh * ATTN_SUB:(hh + 1) * ATTN_SUB].T.astype(BF16))

    @pl.when(pl.program_id(2) == 0)
    def _():
        kf = k_ref[...].astype(F32)
        kn2 = jnp.sum(kf * kf, axis=1, keepdims=True)
        kmax_scr[...] = jnp.broadcast_to(jnp.max(kn2, axis=0, keepdims=True), kmax_scr.shape)

    qs = [load_q(sub) for sub in range(n_sub)]
    ones = jnp.ones((8, HEAD_DIM), BF16)
    shift = []
    for sub in range(n_sub):
        qf = qs[sub].astype(F32)
        qn2 = lax.dot_general(ones, (qf * qf).astype(BF16), qk_dims, preferred_element_type=F32)
        shift.append(jnp.sqrt(qn2[0:1] * kmax_scr[0:1, 0:1]))
    def scores(c, sub):
        k_c = k_ref[c * ATTN_TK:(c + 1) * ATTN_TK, :]
        return lax.dot_general(k_c, qs[sub], qk_dims, preferred_element_type=F32)

    units = [(c, sub) for c in range(n_chunks) for sub in range(n_sub)]
    acc = [None] * n_sub
    s_next = scores(*units[0])
    for u, (c, sub) in enumerate(units):
        s = s_next
        if u + 1 < len(units):
            s_next = scores(*units[u + 1])
        vt_c = vt_ref[:, c * ATTN_TK:(c + 1) * ATTN_TK]
        d = jnp.dot(vt_c, jnp.exp2(s - shift[sub]).astype(BF16), preferred_element_type=F32)
        acc[sub] = d if c == 0 else acc[sub] + d
    den_min = None
    for sub in range(n_sub):
        store_out(sub, acc[sub])
        den = acc[sub][HEAD_DIM:HEAD_DIM + 1]
        den_min = den if den_min is None else jnp.minimum(den_min, den)
    accurate = jnp.min(den_min) >= ATTN_MIN_DENOM

    @pl.when(jnp.logical_not(accurate))
    def _():
        for sub in range(n_sub):
            q = load_q(sub)
            m_scr[...] = jnp.full(m_scr.shape, -jnp.inf, F32)
            acc_scr[...] = jnp.zeros_like(acc_scr)

            def body(c, carry):
                off = pl.multiple_of(c * ATTN_TK, ATTN_TK)
                s = lax.dot_general(k_ref[pl.ds(off, ATTN_TK), :], q, qk_dims, preferred_element_type=F32)
                m_old = m_scr[...]
                m_new = jnp.maximum(m_old, jnp.max(s, axis=0, keepdims=True))
                p = jnp.exp2(s - m_new).astype(BF16)
                acc_scr[...] = (jnp.exp2(m_old - m_new) * acc_scr[...]
                                + jnp.dot(vt_ref[:, pl.ds(off, ATTN_TK)], p, preferred_element_type=F32))
                m_scr[...] = m_new
                return carry

            lax.fori_loop(0, n_chunks, body, 0)
            store_out(sub, acc_scr[...])


def _attention(q, k, vt):
    t_tokens = q.shape[1]
    n_b = t_tokens // SEQ
    qt = SEQ // ATTN_TQ
    return pl.pallas_call(
        _attn_kernel,
        grid=(n_b, N_KV_HEADS, qt),
        in_specs=[
            pl.BlockSpec((Q_PER_KV, ATTN_TQ, HEAD_DIM), lambda b, h, i: (h, b * qt + i, 0)),
            pl.BlockSpec((None, SEQ, HEAD_DIM), lambda b, h, i: (h, b, 0)),
            pl.BlockSpec((None, VT_ROWS, SEQ), lambda b, h, i: (h, 0, b)),
        ],
        out_specs=pl.BlockSpec((ATTN_TQ, Q_PER_KV * HEAD_DIM), lambda b, h, i: (b * qt + i, h)),
        out_shape=jax.ShapeDtypeStruct((t_tokens, ATTN_Q_DIM), BF16),
        scratch_shapes=[
            pltpu.VMEM((8, LANES), F32),
            pltpu.VMEM((1, Q_PER_KV * ATTN_SUB), F32),
            pltpu.VMEM((VT_ROWS, Q_PER_KV * ATTN_SUB), F32),
        ],
        compiler_params=_cparams(("arbitrary", "arbitrary", "arbitrary")),
        name="attention",
    )(q, k, vt)


def _fourier_tables():
    r = FFT_RADIX
    n = FOURIER_GROUP_DIM
    ch = np.arange(n)
    ang_c = 2.0 * np.pi * np.outer(ch, ch) / n
    scale = 1.0 / math.sqrt(SEQ * n)
    w_ch = np.concatenate([np.cos(ang_c), -np.sin(ang_c)], axis=1) * scale
    a = np.arange(r)
    ang1 = 2.0 * np.pi * np.outer(a, a) / r
    c1, s1 = np.cos(ang1), np.sin(ang1)
    w1 = np.zeros((r, 2, 2, r))
    w1[:, 0, 0, :] = c1
    w1[:, 0, 1, :] = s1
    w1[:, 1, 0, :] = -s1
    w1[:, 1, 1, :] = c1
    w1 = w1.reshape(2 * r, 2 * r)
    sp = (np.arange(r)[:, None] + r * np.arange(r)[None, :])
    th = 2.0 * np.pi * sp[:, :, None] * np.arange(r)[None, None, :] / SEQ
    w2 = np.concatenate([np.cos(th), np.sin(th)], axis=2)
    return tuple(jnp.asarray(t, dtype=F32).astype(BF16) for t in (w_ch, w1, w2))


def _fourier_kernel(f_ref, wch_ref, w1_ref, w2_ref, o_ref, x_scr, y_scr, o_scr):
    r = FFT_RADIX
    n = FOURIER_GROUP_DIM
    z = jnp.dot(f_ref[...], wch_ref[...], preferred_element_type=F32)
    x_scr[0:r] = z[:, :n].astype(BF16).reshape(r, r, n)
    x_scr[r:2 * r] = z[:, n:].astype(BF16).reshape(r, r, n)
    y = lax.dot_general(w1_ref[...], x_scr[...], (((1,), (0,)), ((), ())),
                        preferred_element_type=F32)
    y_scr[...] = y.astype(BF16)

    def body(c, carry):
        slab = y_scr[pl.ds(2 * c, 2)].reshape(2 * r, n)
        res = jnp.dot(w2_ref[c], slab, preferred_element_type=F32)
        for s in range(n // LANES):
            o_scr[s, pl.ds(c, r, stride=r), :] = res[:, s * LANES:(s + 1) * LANES]
        return carry

    lax.fori_loop(0, r, body, 0, unroll=FFT_STAGE2_UNROLL)
    for s in range(n // LANES):
        o_ref[:, s * LANES:(s + 1) * LANES] = o_scr[s].astype(BF16)


def _fourier(f):
    t_tokens = f.shape[1]
    n_b = t_tokens // SEQ
    r = FFT_RADIX
    n = FOURIER_GROUP_DIM
    w_ch, w1, w2 = _fourier_tables()
    blk = pl.BlockSpec((None, SEQ, n), lambda g, b: (g, b, 0))
    return pl.pallas_call(
        _fourier_kernel,
        grid=(N_FOURIER_GROUPS, n_b),
        in_specs=[
            blk,
            pl.BlockSpec((n, 2 * n), lambda g, b: (0, 0)),
            pl.BlockSpec((2 * r, 2 * r), lambda g, b: (0, 0)),
            pl.BlockSpec((r, r, 2 * r), lambda g, b: (0, 0, 0)),
        ],
        out_specs=blk,
        out_shape=jax.ShapeDtypeStruct((N_FOURIER_GROUPS, t_tokens, n), BF16),
        scratch_shapes=[
            pltpu.VMEM((2 * r, r, n), BF16),
            pltpu.VMEM((2 * r, r, n), BF16),
            pltpu.VMEM((n // LANES, SEQ, LANES), F32),
        ],
        compiler_params=_cparams(("arbitrary", "arbitrary")),
        name="fourier",
    )(f, w_ch, w1, w2)


def _post_mix_kernel(attn_ref, four_ref, gate_ref, x_ref, mod_ref, gpost_ref, gpre_ref,
                     wao_ref, wfo_ref, wmo_ref, wr_ref, br_ref,
                     x1_ref, h2_ref, logit_ref):
    a = jnp.dot(attn_ref[...], wao_ref[...], preferred_element_type=F32)
    fo = jnp.dot(four_ref[0], wfo_ref[0], preferred_element_type=F32)
    for g in range(1, N_FOURIER_GROUPS):
        fo = fo + jnp.dot(four_ref[g], wfo_ref[g], preferred_element_type=F32)
    ga = gate_ref[:, :D_MODEL].astype(F32)
    gf = gate_ref[:, D_MODEL:].astype(F32)
    m = (ga * a + gf * fo).astype(BF16)
    mixed = jnp.dot(m, wmo_ref[...], preferred_element_type=F32)
    ms = jnp.mean(mixed * mixed, axis=-1, keepdims=True)
    y = mixed * lax.rsqrt(ms + NORM_EPS) * gpost_ref[...]
    x1 = x_ref[...] + mod_ref[2:3, :] * y
    x1_ref[...] = x1
    ms1 = jnp.mean(x1 * x1, axis=-1, keepdims=True)
    h2 = (x1 * lax.rsqrt(ms1 + NORM_EPS) * gpre_ref[...]) * (1.0 + mod_ref[4:5, :]) + mod_ref[3:4, :]
    h2_ref[...] = h2
    logit_ref[...] = jnp.dot(h2.astype(BF16), wr_ref[...], preferred_element_type=F32) + br_ref[...]


def _post_mix(attn, four, gates, x2, mod3, g_post, g_pre, wao, wfo4, wmo, w_router, b_router):
    t_tokens = x2.shape[0]
    tm = POST_TM
    tiles_per_seq = SEQ // tm
    n = FOURIER_GROUP_DIM
    const2 = lambda i: (0, 0)
    resident = dict(pipeline_mode=pl.Buffered(1))
    return pl.pallas_call(
        _post_mix_kernel,
        grid=(t_tokens // tm,),
        in_specs=[
            pl.BlockSpec((tm, ATTN_Q_DIM), lambda i: (i, 0)),
            pl.BlockSpec((N_FOURIER_GROUPS, tm, n), lambda i: (0, i, 0)),
            pl.BlockSpec((tm, 2 * D_MODEL), lambda i: (i, 0)),
            pl.BlockSpec((tm, D_MODEL), lambda i: (i, 0)),
            pl.BlockSpec((None, 6, D_MODEL), lambda i: (i // tiles_per_seq, 0, 0)),
            pl.BlockSpec((1, D_MODEL), const2),
            pl.BlockSpec((1, D_MODEL), const2),
            pl.BlockSpec((ATTN_Q_DIM, D_MODEL), const2, **resident),
            pl.BlockSpec((N_FOURIER_GROUPS, n, D_MODEL), lambda i: (0, 0, 0), **resident),
            pl.BlockSpec((D_MODEL, D_MODEL), const2, **resident),
            pl.BlockSpec((D_MODEL, LANES), const2, **resident),
            pl.BlockSpec((1, LANES), const2),
        ],
        out_specs=[
            pl.BlockSpec((tm, D_MODEL), lambda i: (i, 0)),
            pl.BlockSpec((tm, D_MODEL), lambda i: (i, 0)),
            pl.BlockSpec((tm, LANES), lambda i: (i, 0)),
        ],
        out_shape=[
            jax.ShapeDtypeStruct((t_tokens, D_MODEL), F32),
            jax.ShapeDtypeStruct((t_tokens, D_MODEL), F32),
            jax.ShapeDtypeStruct((t_tokens, LANES), F32),
        ],
        compiler_params=_cparams(("arbitrary",)),
        name="post_mix",
    )(attn, four, gates, x2, mod3, g_post.reshape(1, D_MODEL), g_pre.reshape(1, D_MODEL),
      wao, wfo4, wmo, w_router, b_router)


def _first_argmax(vals, lane):
    mx = jnp.max(vals, axis=-1, keepdims=True)
    idx = jnp.min(jnp.where(vals == mx, lane, float(LANES)), axis=-1, keepdims=True)
    return mx, idx


def _route_kernel(logit_ref, tri_ref, out_ref, cnt_ref, carry_scr, start_scr):
    phase = pl.program_id(0)
    i = pl.program_id(1)

    @pl.when((phase == 0) & (i == 0))
    def _():
        carry_scr[...] = jnp.zeros_like(carry_scr)
        start_scr[...] = jnp.zeros_like(start_scr)

    @pl.when((phase == 1) & (i == 0))
    def _():
        counts = carry_scr[...]
        lane8 = lax.broadcasted_iota(jnp.int32, counts.shape, 1)
        incl = counts
        shift = 1
        while shift < N_EXPERTS:
            incl = incl + jnp.where(lane8 >= N_GROUPS + shift, pltpu.roll(incl, shift, 1), 0.0)
            shift *= 2
        start_scr[...] = incl - counts
        carry_scr[...] = jnp.zeros_like(carry_scr)

    lg = logit_ref[...]
    lane = lax.broadcasted_iota(jnp.int32, lg.shape, 1).astype(F32)
    neg = jnp.float32(-jnp.inf)
    gl = jnp.where(lane < N_GROUPS, lg, neg)
    gmax, gidx = _first_argmax(gl, lane)
    g_w = 1.0 / jnp.sum(jnp.exp(gl - gmax), axis=-1, keepdims=True)
    lo = N_GROUPS + EXPERTS_PER_GROUP * gidx
    el = jnp.where((lane >= lo) & (lane < lo + EXPERTS_PER_GROUP), lg, neg)
    m1, i1 = _first_argmax(el, lane)
    el2 = jnp.where(lane == i1, neg, el)
    m2, i2 = _first_argmax(el2, lane)
    p2 = jnp.exp(m2 - m1)
    w1 = g_w / (1.0 + p2)
    w2 = g_w * p2 / (1.0 + p2)
    oh = jnp.where((lane == i1) | (lane == i2), 1.0, 0.0)
    before = (jnp.dot(tri_ref[...], oh.astype(BF16), preferred_element_type=F32)
              + carry_scr[0:1, :] + start_scr[0:1, :])
    pos1 = jnp.sum(jnp.where(lane == i1, before, 0.0), axis=-1, keepdims=True)
    pos2 = jnp.sum(jnp.where(lane == i2, before, 0.0), axis=-1, keepdims=True)
    carry_scr[...] = carry_scr[...] + jnp.sum(oh, axis=0, keepdims=True)
    out = jnp.where(lane == 0, i1 - N_GROUPS, 0.0)
    out = jnp.where(lane == 1, i2 - N_GROUPS, out)
    out = jnp.where(lane == 2, pos1, out)
    out = jnp.where(lane == 3, pos2, out)
    out = jnp.where(lane == 4, w1, out)
    out = jnp.where(lane == 5, w2, out)
    out_ref[...] = out
    cnt_ref[...] = carry_scr[...]


def _route(logits):
    t_tokens = logits.shape[0]
    tb = ROUTE_TB
    tri = jnp.asarray(np.tril(np.ones((tb, tb), np.float32), -1), dtype=BF16)
    return pl.pallas_call(
        _route_kernel,
        grid=(2, t_tokens // tb),
        in_specs=[
            pl.BlockSpec((tb, LANES), lambda p, i: (i, 0)),
            pl.BlockSpec((tb, tb), lambda p, i: (0, 0)),
        ],
        out_specs=[
            pl.BlockSpec((tb, LANES), lambda p, i: (i * p, 0)),
            pl.BlockSpec((8, LANES), lambda p, i: (0, 0)),
        ],
        out_shape=[
            jax.ShapeDtypeStruct((t_tokens, LANES), F32),
            jax.ShapeDtypeStruct((8, LANES), F32),
        ],
        scratch_shapes=[pltpu.VMEM((8, LANES), F32), pltpu.VMEM((8, LANES), F32)],
        compiler_params=_cparams(("arbitrary", "arbitrary")),
        name="route",
    )(logits, tri)


def _row_gather(src_hbm, src_row, j, dst, sem):
    return pltpu.make_async_copy(src_hbm.at[pl.ds(src_row, 1), :], dst.at[pl.ds(j, 1), :], sem)


def _start_row_gathers(src_hbm, idx_ref, dst, sem, n_rows):
    def body(jj, carry):
        for u in range(GATHER_UNROLL):
            j = jj * GATHER_UNROLL + u
            _row_gather(src_hbm, idx_ref[0, 0, j], j, dst, sem).start(priority=u % 2)
        return carry

    lax.fori_loop(0, n_rows // GATHER_UNROLL, body, 0)


def _wait_row_gathers(src_hbm, dst, sem, n_rows):
    def body(jj, carry):
        for u in range(GATHER_UNROLL):
            _row_gather(src_hbm, 0, jj * GATHER_UNROLL + u, dst, sem).wait()
        return carry

    lax.fori_loop(0, n_rows // GATHER_UNROLL, body, 0)


def _row_scatter(src, j, dst_hbm, dst_row, sem):
    return pltpu.make_async_copy(src.at[pl.ds(j, 1), :], dst_hbm.at[pl.ds(dst_row, 1), :], sem)


def _dispatch_kernel(p1_ref, p2_ref, h_ref, xs_hbm, buf, sem):
    i = pl.program_id(0)
    slot = i % 2
    src = buf.at[slot]
    src[...] = h_ref[...]

    def start(jj, carry):
        for u in range(GATHER_UNROLL):
            j = jj * GATHER_UNROLL + u
            _row_scatter(src, j, xs_hbm, p1_ref[0, 0, j], sem.at[slot]).start(priority=0)
            _row_scatter(src, j, xs_hbm, p2_ref[0, 0, j], sem.at[slot]).start(priority=1)
        return carry

    lax.fori_loop(0, MOE_TB // GATHER_UNROLL, start, 0)

    def wait_block(s):
        def body(jj, carry):
            for u in range(GATHER_UNROLL):
                j = jj * GATHER_UNROLL + u
                _row_scatter(buf.at[s], j, xs_hbm, 0, sem.at[s]).wait()
                _row_scatter(buf.at[s], j, xs_hbm, 0, sem.at[s]).wait()
            return carry

        lax.fori_loop(0, MOE_TB // GATHER_UNROLL, body, 0)

    @pl.when(i > 0)
    def _():
        wait_block(1 - slot)

    @pl.when(i + 1 == pl.num_programs(0))
    def _():
        wait_block(slot)


def _dispatch(pos1, pos2, h2):
    t_tokens = h2.shape[0]
    nb = t_tokens // MOE_TB
    smem_blk = pl.BlockSpec((1, 1, MOE_TB), lambda i: (i, 0, 0), memory_space=pltpu.SMEM)
    return pl.pallas_call(
        _dispatch_kernel,
        grid=(nb,),
        in_specs=[smem_blk, smem_blk, pl.BlockSpec((MOE_TB, D_MODEL), lambda i: (i, 0))],
        out_specs=pl.BlockSpec(memory_space=pl.ANY),
        out_shape=jax.ShapeDtypeStruct((2 * t_tokens, D_MODEL), F32),
        scratch_shapes=[pltpu.VMEM((2, MOE_TB, D_MODEL), F32), pltpu.SemaphoreType.DMA((2,))],
        compiler_params=_cparams(("arbitrary",)),
        name="dispatch",
    )(pos1.reshape(nb, 1, MOE_TB), pos2.reshape(nb, 1, MOE_TB), h2)


def _expert_kernel(tile_ref, exp_ref, start_ref, end_ref, nitem_ref,
                   xs_ref, wg_ref, wu_ref, wd_ref, ys_ref, wg_scr, wu_scr, wd_scr):
    k = pl.program_id(0)
    tile = tile_ref[k]
    e = exp_ref[k]
    prev = jnp.maximum(k - 1, 0)
    first_visit = (k == 0) | (tile != tile_ref[prev])
    new_expert = (k == 0) | (e != exp_ref[prev])

    @pl.when(new_expert)
    def _():
        wg_scr[...] = wg_ref[...].astype(BF16)
        wu_scr[...] = wu_ref[...].astype(BF16)
        wd_scr[...] = wd_ref[...].astype(BF16)

    @pl.when(first_visit)
    def _():
        ys_ref[...] = jnp.zeros_like(ys_ref)

    @pl.when(k < nitem_ref[0])
    def _():
        x = xs_ref[...].astype(BF16)
        g = jnp.dot(x, wg_scr[...], preferred_element_type=F32)
        u = jnp.dot(x, wu_scr[...], preferred_element_type=F32)
        hmid = (g * jax.nn.sigmoid(g) * u).astype(BF16)
        y = jnp.dot(hmid, wd_scr[...], preferred_element_type=F32)
        row = tile * MOE_TM + lax.broadcasted_iota(jnp.int32, (MOE_TM, 1), 0)
        valid = (row >= start_ref[e]) & (row < end_ref[e])
        ys_ref[...] += jnp.where(valid, y, 0.0)


def _experts(item_tile, item_exp, starts, ends, n_items, xs, wg, wu, wd):
    n_rows = xs.shape[0]
    n_tiles = n_rows // MOE_TM
    max_items = n_tiles + N_EXPERTS - 1
    grid_spec = pltpu.PrefetchScalarGridSpec(
        num_scalar_prefetch=5,
        grid=(max_items,),
        in_specs=[
            pl.BlockSpec((MOE_TM, D_MODEL), lambda k, t, e, s, en, n: (t[k], 0)),
            pl.BlockSpec((None, D_MODEL, EXPERT_FF), lambda k, t, e, s, en, n: (e[k], 0, 0)),
            pl.BlockSpec((None, D_MODEL, EXPERT_FF), lambda k, t, e, s, en, n: (e[k], 0, 0)),
            pl.BlockSpec((None, EXPERT_FF, D_MODEL), lambda k, t, e, s, en, n: (e[k], 0, 0)),
        ],
        out_specs=pl.BlockSpec((MOE_TM, D_MODEL), lambda k, t, e, s, en, n: (t[k], 0)),
        scratch_shapes=[
            pltpu.VMEM((D_MODEL, EXPERT_FF), BF16),
            pltpu.VMEM((D_MODEL, EXPERT_FF), BF16),
            pltpu.VMEM((EXPERT_FF, D_MODEL), BF16),
        ],
    )
    return pl.pallas_call(
        _expert_kernel,
        grid_spec=grid_spec,
        out_shape=jax.ShapeDtypeStruct((n_rows, D_MODEL), F32),
        compiler_params=_cparams(("arbitrary",)),
        name="experts",
    )(item_tile, item_exp, starts, ends, n_items, xs, wg, wu, wd)


def _expert_items(counts):
    ends = jnp.cumsum(counts)
    starts = ends - counts
    first_tile = starts // MOE_TM
    last_tile = (ends - 1) // MOE_TM
    n_it = jnp.where(counts > 0, last_tile - first_tile + 1, 0)
    it_end = jnp.cumsum(n_it)
    it_start = it_end - n_it
    total = it_end[-1]
    return starts, ends, first_tile, it_start, it_end, total


def _combine_kernel(p1_ref, p2_ref, p1_next_ref, p2_next_ref, ys_hbm, route_ref, x1_ref, mod_ref, g_ref,
                    o_ref, y_scr, sem):
    i = pl.program_id(0)
    slot = i % 2

    def start_block(pa_ref, pb_ref, s):
        _start_row_gathers(ys_hbm, pa_ref, y_scr.at[s, 0], sem.at[s], MOE_TB)
        _start_row_gathers(ys_hbm, pb_ref, y_scr.at[s, 1], sem.at[s], MOE_TB)

    @pl.when(i == 0)
    def _():
        start_block(p1_ref, p2_ref, 0)

    @pl.when(i + 1 < pl.num_programs(0))
    def _():
        start_block(p1_next_ref, p2_next_ref, 1 - slot)

    _wait_row_gathers(ys_hbm, y_scr.at[slot, 0], sem.at[slot], MOE_TB)
    _wait_row_gathers(ys_hbm, y_scr.at[slot, 1], sem.at[slot], MOE_TB)
    w1 = route_ref[:, 4:5]
    w2 = route_ref[:, 5:6]
    ffn = w1 * y_scr[slot, 0] + w2 * y_scr[slot, 1]
    ms = jnp.mean(ffn * ffn, axis=-1, keepdims=True)
    y = ffn * lax.rsqrt(ms + NORM_EPS) * g_ref[...]
    o_ref[...] = x1_ref[...] + mod_ref[5:6, :] * y


def _combine(pos1, pos2, ys, route, x1, mod3, g_post):
    t_tokens = x1.shape[0]
    tb = MOE_TB
    nb = t_tokens // tb
    tiles_per_seq = SEQ // tb
    smem_blk = pl.BlockSpec((1, 1, tb), lambda i: (i, 0, 0), memory_space=pltpu.SMEM)
    smem_next = pl.BlockSpec((1, 1, tb), lambda i: (jnp.minimum(i + 1, nb - 1), 0, 0), memory_space=pltpu.SMEM)
    p1 = pos1.reshape(nb, 1, tb)
    p2 = pos2.reshape(nb, 1, tb)
    return pl.pallas_call(
        _combine_kernel,
        grid=(nb,),
        in_specs=[
            smem_blk, smem_blk, smem_next, smem_next,
            pl.BlockSpec(memory_space=pl.ANY),
            pl.BlockSpec((tb, LANES), lambda i: (i, 0)),
            pl.BlockSpec((tb, D_MODEL), lambda i: (i, 0)),
            pl.BlockSpec((None, 6, D_MODEL), lambda i: (i // tiles_per_seq, 0, 0)),
            pl.BlockSpec((1, D_MODEL), lambda i: (0, 0)),
        ],
        out_specs=pl.BlockSpec((tb, D_MODEL), lambda i: (i, 0)),
        out_shape=jax.ShapeDtypeStruct((t_tokens, D_MODEL), F32),
        scratch_shapes=[
            pltpu.VMEM((2, 2, tb, D_MODEL), F32),
            pltpu.SemaphoreType.DMA((2,)),
        ],
        compiler_params=_cparams(("arbitrary",)),
        name="combine",
    )(p1, p2, p1, p2, ys, route, x1, mod3, g_post.reshape(1, D_MODEL))


def _rope_tables():
    rows = SEQ // GRID_W
    row = jnp.repeat(jnp.arange(rows, dtype=F32), GRID_W)
    col = jnp.tile(jnp.arange(GRID_W, dtype=F32), rows)
    n_freq = HEAD_DIM // 4
    inv = ROPE_THETA ** (-jnp.arange(n_freq, dtype=F32) / n_freq)
    ang_r = row[:, None] * inv
    ang_c = col[:, None] * inv
    ang = jnp.concatenate([ang_r, ang_r, ang_c, ang_c], axis=-1)
    cos, sin = jnp.cos(ang), jnp.sin(ang)
    first = (jnp.arange(HEAD_DIM) % (2 * n_freq)) < n_freq
    sin_a = jnp.where(first, -sin, 0.0)
    sin_b = jnp.where(first, 0.0, sin)
    return cos, sin_a, sin_b


def kernel(x, c, w_ada, b_ada, g_pre_mix, g_post_mix, w_in, q_norm_g, k_norm_g, w_attn_out, w_fourier_out,
           w_mix_out, g_pre_ffn, g_post_ffn, w_group_router, b_group_router, w_expert_router, b_expert_router,
           w_exp_gate, w_exp_up, w_exp_down):
    n_b, seq, d = x.shape
    assert seq == SEQ and d == D_MODEL
    t_tokens = n_b * seq
    x2 = x.reshape(t_tokens, d)

    c_pad = jnp.zeros((8, d), F32).at[:n_b].set(c)
    mod = _ada(c_pad, w_ada, b_ada)[:n_b]
    mod3 = mod.reshape(n_b, 6, d)

    cos, sin_a, sin_b = _rope_tables()
    q, k, vt, f, gates = _in_proj(x2, mod3, g_pre_mix, _regroup_w_in(w_in), cos, sin_a, sin_b, q_norm_g, k_norm_g)
    attn = _attention(q, k, vt)
    four = _fourier(f)

    n_r = N_GROUPS + N_EXPERTS
    w_router = jnp.zeros((d, LANES), F32).at[:, :N_GROUPS].set(w_group_router).at[:, N_GROUPS:n_r].set(w_expert_router)
    b_router = jnp.zeros((1, LANES), F32).at[0, :N_GROUPS].set(b_group_router).at[0, N_GROUPS:n_r].set(b_expert_router)
    x1, h2, logits = _post_mix(
        attn, four, gates, x2, mod3, g_post_mix, g_pre_ffn,
        w_attn_out.astype(BF16),
        w_fourier_out.astype(BF16).reshape(N_FOURIER_GROUPS, FOURIER_GROUP_DIM, d),
        w_mix_out.astype(BF16), w_router.astype(BF16), b_router)

    route, cnt = _route(logits)
    counts = cnt[0, N_GROUPS:n_r].astype(jnp.int32)
    starts, ends, first_tile, it_start, it_end, total = _expert_items(counts)
    pos1 = route[:, 2].astype(jnp.int32)
    pos2 = route[:, 3].astype(jnp.int32)

    xs = _dispatch(pos1, pos2, h2)

    n_tiles = 2 * t_tokens // MOE_TM
    max_items = n_tiles + N_EXPERTS - 1
    kk = jnp.minimum(jnp.arange(max_items, dtype=jnp.int32), total - 1)
    item_exp = jnp.sum((it_end[None, :] <= kk[:, None]).astype(jnp.int32), axis=1)
    item_tile = first_tile[item_exp] + kk - it_start[item_exp]
    ys = _experts(item_tile.astype(jnp.int32), item_exp.astype(jnp.int32), starts.astype(jnp.int32),
                  ends.astype(jnp.int32), total.reshape(1).astype(jnp.int32), xs, w_exp_gate, w_exp_up, w_exp_down)

    out = _combine(pos1, pos2, ys, route, x1, mod3, g_post_ffn)
    return out.reshape(n_b, seq, d)
```

```python
import math

import numpy as np
import jax
import jax.numpy as jnp
from jax import lax
from jax.experimental import pallas as pl
from jax.experimental.pallas import tpu as pltpu

F32 = jnp.float32
BF16 = jnp.bfloat16

D_MODEL = 2048
SEQ = 4096
N_HEADS = 16
N_KV_HEADS = 4
Q_PER_KV = N_HEADS // N_KV_HEADS
HEAD_DIM = 128
ROPE_THETA = 10000.0
GRID_W = 64
ATTN_Q_DIM = N_HEADS * HEAD_DIM
ATTN_KV_DIM = N_KV_HEADS * HEAD_DIM
N_FOURIER_GROUPS = 4
FOURIER_GROUP_DIM = 256
FOURIER_DIM = N_FOURIER_GROUPS * FOURIER_GROUP_DIM
IN_PROJ_DIM = ATTN_Q_DIM + 2 * ATTN_KV_DIM + FOURIER_DIM + 2 * D_MODEL
N_GROUPS = 4
EXPERTS_PER_GROUP = 8
N_EXPERTS = N_GROUPS * EXPERTS_PER_GROUP
EXPERT_FF = 512
NORM_EPS = 1e-6

LANES = 128
FFT_RADIX = 64
FFT_STAGE2_UNROLL = 8
VMEM_LIMIT = 56 * 1024 * 1024

Q_PRESCALE = HEAD_DIM ** -0.5 * math.log2(math.e)

ADA_TN = 1024
INPROJ_TM = 512
INPROJ_GROUPS = N_KV_HEADS
INPROJ_Q_COLS = ATTN_Q_DIM // INPROJ_GROUPS
INPROJ_GATE_COLS = 2 * D_MODEL // INPROJ_GROUPS
INPROJ_GATE_SPLIT = 512
VT_ROWS = HEAD_DIM + 16
ATTN_TQ = 2048
ATTN_SUB = 128
ATTN_TK = 512
ATTN_MIN_DENOM = 2.0 ** -64
POST_TM = 256
ROUTE_TB = 1024
MOE_TB = 512
MOE_TM = 512
GATHER_UNROLL = 8


def _cparams(sem, vmem=VMEM_LIMIT):
    return pltpu.CompilerParams(dimension_semantics=sem, vmem_limit_bytes=vmem)


def _ada_kernel(c_ref, w_ref, b_ref, o_ref):
    cs = c_ref[...]
    s = cs * jax.nn.sigmoid(cs)
    o_ref[...] = jnp.dot(s.astype(BF16), w_ref[...].astype(BF16), preferred_element_type=F32) + b_ref[...]


def _ada(c_pad, w_ada, b_ada):
    n = w_ada.shape[1]
    return pl.pallas_call(
        _ada_kernel,
        grid=(n // ADA_TN,),
        in_specs=[
            pl.BlockSpec((8, D_MODEL), lambda j: (0, 0)),
            pl.BlockSpec((D_MODEL, ADA_TN), lambda j: (0, j)),
            pl.BlockSpec((1, ADA_TN), lambda j: (0, j)),
        ],
        out_specs=pl.BlockSpec((8, ADA_TN), lambda j: (0, j)),
        out_shape=jax.ShapeDtypeStruct((8, n), F32),
        compiler_params=_cparams(("arbitrary",)),
        name="ada",
    )(c_pad, w_ada, b_ada.reshape(1, n))


def _head_norm_rope(t, g, cos, sin_a, sin_b):
    ms = jnp.mean(t * t, axis=-1, keepdims=True)
    y = t * lax.rsqrt(ms + NORM_EPS) * g
    return y * cos + pltpu.roll(y, 96, 1) * sin_a + pltpu.roll(y, 32, 1) * sin_b


def _inproj_kernel(x_ref, mod_ref, g_ref, w_ref, cos_ref, sa_ref, sb_ref, qg_ref, kg_ref,
                   q_ref, k_ref, v_ref, f_ref, gate_ref, h_scr):
    j = pl.program_id(1)

    @pl.when(j == 0)
    def _():
        x = x_ref[...]
        ms = jnp.mean(x * x, axis=-1, keepdims=True)
        y = x * lax.rsqrt(ms + NORM_EPS) * g_ref[...]
        h_scr[...] = (y * (1.0 + mod_ref[1:2, :]) + mod_ref[0:1, :]).astype(BF16)

    h = h_scr[...]
    cos, sin_a, sin_b = cos_ref[...], sa_ref[...], sb_ref[...]
    c0 = 0
    r_q = jnp.dot(h, w_ref[:, c0:c0 + INPROJ_Q_COLS], preferred_element_type=F32)
    for hh in range(INPROJ_Q_COLS // HEAD_DIM):
        t = _head_norm_rope(r_q[:, hh * HEAD_DIM:(hh + 1) * HEAD_DIM], qg_ref[...], cos, sin_a, sin_b)
        q_ref[hh] = (t * Q_PRESCALE).astype(BF16)
    c0 += INPROJ_Q_COLS
    r_kv = jnp.dot(h, w_ref[:, c0:c0 + 2 * HEAD_DIM], preferred_element_type=F32)
    k_ref[0] = _head_norm_rope(r_kv[:, :HEAD_DIM], kg_ref[...], cos, sin_a, sin_b).astype(BF16)
    v_ref[0, 0:HEAD_DIM, :] = r_kv[:, HEAD_DIM:].T.astype(BF16)
    ones_row = lax.broadcasted_iota(jnp.int32, (VT_ROWS - HEAD_DIM, INPROJ_TM), 0) == 0
    v_ref[0, HEAD_DIM:VT_ROWS, :] = jnp.where(ones_row, 1.0, 0.0).astype(BF16)
    c0 += 2 * HEAD_DIM
    f_ref[0] = jnp.dot(h, w_ref[:, c0:c0 + FOURIER_GROUP_DIM], preferred_element_type=F32).astype(BF16)
    c0 += FOURIER_GROUP_DIM
    for s in range(INPROJ_GATE_COLS // INPROJ_GATE_SPLIT):
        r_g = jnp.dot(h, w_ref[:, c0:c0 + INPROJ_GATE_SPLIT], preferred_element_type=F32)
        gate_ref[:, s * INPROJ_GATE_SPLIT:(s + 1) * INPROJ_GATE_SPLIT] = jax.nn.sigmoid(r_g).astype(BF16)
        c0 += INPROJ_GATE_SPLIT


def _regroup_w_in(w_in):
    o1 = ATTN_Q_DIM
    o2 = o1 + ATTN_KV_DIM
    o3 = o2 + ATTN_KV_DIM
    o4 = o3 + FOURIER_DIM
    parts = []
    for j in range(INPROJ_GROUPS):
        parts += [
            w_in[:, j * INPROJ_Q_COLS:(j + 1) * INPROJ_Q_COLS],
            w_in[:, o1 + j * HEAD_DIM:o1 + (j + 1) * HEAD_DIM],
            w_in[:, o2 + j * HEAD_DIM:o2 + (j + 1) * HEAD_DIM],
            w_in[:, o3 + j * FOURIER_GROUP_DIM:o3 + (j + 1) * FOURIER_GROUP_DIM],
            w_in[:, o4 + j * INPROJ_GATE_COLS:o4 + (j + 1) * INPROJ_GATE_COLS],
        ]
    return jnp.concatenate(parts, axis=1).astype(BF16)


def _in_proj(x2, mod3, g_pre, w_grouped, cos, sin_a, sin_b, qg, kg):
    t_tokens = x2.shape[0]
    tm = INPROJ_TM
    tn = IN_PROJ_DIM // INPROJ_GROUPS
    tiles_per_seq = SEQ // tm
    row_tbl = pl.BlockSpec((tm, HEAD_DIM), lambda i, j: (i % tiles_per_seq, 0))
    vec_hd = pl.BlockSpec((1, HEAD_DIM), lambda i, j: (0, 0))
    return pl.pallas_call(
        _inproj_kernel,
        grid=(t_tokens // tm, INPROJ_GROUPS),
        in_specs=[
            pl.BlockSpec((tm, D_MODEL), lambda i, j: (i, 0)),
            pl.BlockSpec((None, 6, D_MODEL), lambda i, j: (i // tiles_per_seq, 0, 0)),
            pl.BlockSpec((1, D_MODEL), lambda i, j: (0, 0)),
            pl.BlockSpec((D_MODEL, tn), lambda i, j: (0, j)),
            row_tbl, row_tbl, row_tbl, vec_hd, vec_hd,
        ],
        out_specs=[
            pl.BlockSpec((INPROJ_Q_COLS // HEAD_DIM, tm, HEAD_DIM), lambda i, j: (j, i, 0)),
            pl.BlockSpec((1, tm, HEAD_DIM), lambda i, j: (j, i, 0)),
            pl.BlockSpec((1, VT_ROWS, tm), lambda i, j: (j, 0, i)),
            pl.BlockSpec((1, tm, FOURIER_GROUP_DIM), lambda i, j: (j, i, 0)),
            pl.BlockSpec((tm, INPROJ_GATE_COLS), lambda i, j: (i, j)),
        ],
        out_shape=[
            jax.ShapeDtypeStruct((N_HEADS, t_tokens, HEAD_DIM), BF16),
            jax.ShapeDtypeStruct((N_KV_HEADS, t_tokens, HEAD_DIM), BF16),
            jax.ShapeDtypeStruct((N_KV_HEADS, VT_ROWS, t_tokens), BF16),
            jax.ShapeDtypeStruct((N_FOURIER_GROUPS, t_tokens, FOURIER_GROUP_DIM), BF16),
            jax.ShapeDtypeStruct((t_tokens, 2 * D_MODEL), BF16),
        ],
        scratch_shapes=[pltpu.VMEM((tm, D_MODEL), BF16)],
        compiler_params=_cparams(("arbitrary", "arbitrary")),
        name="in_proj",
    )(x2, mod3, g_pre.reshape(1, D_MODEL), w_grouped, cos, sin_a, sin_b,
      qg.reshape(1, HEAD_DIM), kg.reshape(1, HEAD_DIM))


def _attn_kernel(q_ref, k_ref, vt_ref, o_ref, kmax_scr, m_scr, acc_scr):
    n_chunks = SEQ // ATTN_TK
    n_sub = ATTN_TQ // ATTN_SUB
    n_q = Q_PER_KV * ATTN_SUB
    qk_dims = (((1,), (1,)), ((), ()))

    def load_q(sub):
        return q_ref[:, sub * ATTN_SUB:(sub + 1) * ATTN_SUB, :].reshape(n_q, HEAD_DIM)

    def store_out(sub, acc):
        o_t = acc[:HEAD_DIM] * (1.0 / acc[HEAD_DIM:HEAD_DIM + 1])
        r0 = sub * ATTN_SUB
        for hh in range(Q_PER_KV):
            o_ref[r0:r0 + ATTN_SUB, hh * HEAD_DIM:(hh + 1) * HEAD_DIM] = (
                o_t[:, hh * ATTN_SUB:(hh + 1) * ATTN_SUB].T.astype(BF16))

    @pl.when(pl.program_id(2) == 0)
    def _():
        kf = k_ref[...].astype(F32)
        kn2 = jnp.sum(kf * kf, axis=1, keepdims=True)
        kmax_scr[...] = jnp.broadcast_to(jnp.max(kn2, axis=0, keepdims=True), kmax_scr.shape)

    qs = [load_q(sub) for sub in range(n_sub)]
    ones = jnp.ones((8, HEAD_DIM), BF16)
    shift = []
    for sub in range(n_sub):
        qf = qs[sub].astype(F32)
        qn2 = lax.dot_general(ones, (qf * qf).astype(BF16), qk_dims, preferred_element_type=F32)
        shift.append(jnp.sqrt(qn2[0:1] * kmax_scr[0:1, 0:1]))
    def scores(c, sub):
        k_c = k_ref[c * ATTN_TK:(c + 1) * ATTN_TK, :]
        return lax.dot_general(k_c, qs[sub], qk_dims, preferred_element_type=F32)

    units = [(c, sub) for c in range(n_chunks) for sub in range(n_sub)]
    acc = [None] * n_sub
    s_next = scores(*units[0])
    for u, (c, sub) in enumerate(units):
        s = s_next
        if u + 1 < len(units):
            s_next = scores(*units[u + 1])
        vt_c = vt_ref[:, c * ATTN_TK:(c + 1) * ATTN_TK]
        d = jnp.dot(vt_c, jnp.exp2(s - shift[sub]).astype(BF16), preferred_element_type=F32)
        acc[sub] = d if c == 0 else acc[sub] + d
    den_min = None
    for sub in range(n_sub):
        store_out(sub, acc[sub])
        den = acc[sub][HEAD_DIM:HEAD_DIM + 1]
        den_min = den if den_min is None else jnp.minimum(den_min, den)
    accurate = jnp.min(den_min) >= ATTN_MIN_DENOM

    @pl.when(jnp.logical_not(accurate))
    def _():
        for sub in range(n_sub):
            q = load_q(sub)
            m_scr[...] = jnp.full(m_scr.shape, -jnp.inf, F32)
            acc_scr[...] = jnp.zeros_like(acc_scr)

            def body(c, carry):
                off = pl.multiple_of(c * ATTN_TK, ATTN_TK)
                s = lax.dot_general(k_ref[pl.ds(off, ATTN_TK), :], q, qk_dims, preferred_element_type=F32)
                m_old = m_scr[...]
                m_new = jnp.maximum(m_old, jnp.max(s, axis=0, keepdims=True))
                p = jnp.exp2(s - m_new).astype(BF16)
                acc_scr[...] = (jnp.exp2(m_old - m_new) * acc_scr[...]
                                + jnp.dot(vt_ref[:, pl.ds(off, ATTN_TK)], p, preferred_element_type=F32))
                m_scr[...] = m_new
                return carry

            lax.fori_loop(0, n_chunks, body, 0)
            store_out(sub, acc_scr[...])


def _attention(q, k, vt):
    t_tokens = q.shape[1]
    n_b = t_tokens // SEQ
    qt = SEQ // ATTN_TQ
    return pl.pallas_call(
        _attn_kernel,
        grid=(n_b, N_KV_HEADS, qt),
        in_specs=[
            pl.BlockSpec((Q_PER_KV, ATTN_TQ, HEAD_DIM), lambda b, h, i: (h, b * qt + i, 0)),
            pl.BlockSpec((None, SEQ, HEAD_DIM), lambda b, h, i: (h, b, 0)),
            pl.BlockSpec((None, VT_ROWS, SEQ), lambda b, h, i: (h, 0, b)),
        ],
        out_specs=pl.BlockSpec((ATTN_TQ, Q_PER_KV * HEAD_DIM), lambda b, h, i: (b * qt + i, h)),
        out_shape=jax.ShapeDtypeStruct((t_tokens, ATTN_Q_DIM), BF16),
        scratch_shapes=[
            pltpu.VMEM((8, LANES), F32),
            pltpu.VMEM((1, Q_PER_KV * ATTN_SUB), F32),
            pltpu.VMEM((VT_ROWS, Q_PER_KV * ATTN_SUB), F32),
        ],
        compiler_params=_cparams(("arbitrary", "arbitrary", "arbitrary")),
        name="attention",
    )(q, k, vt)


def _fourier_tables():
    r = FFT_RADIX
    n = FOURIER_GROUP_DIM
    ch = np.arange(n)
    ang_c = 2.0 * np.pi * np.outer(ch, ch) / n
    scale = 1.0 / math.sqrt(SEQ * n)
    w_ch = np.concatenate([np.cos(ang_c), -np.sin(ang_c)], axis=1) * scale
    a = np.arange(r)
    ang1 = 2.0 * np.pi * np.outer(a, a) / r
    c1, s1 = np.cos(ang1), np.sin(ang1)
    w1 = np.zeros((r, 2, 2, r))
    w1[:, 0, 0, :] = c1
    w1[:, 0, 1, :] = s1
    w1[:, 1, 0, :] = -s1
    w1[:, 1, 1, :] = c1
    w1 = w1.reshape(2 * r, 2 * r)
    sp = (np.arange(r)[:, None] + r * np.arange(r)[None, :])
    th = 2.0 * np.pi * sp[:, :, None] * np.arange(r)[None, None, :] / SEQ
    w2 = np.concatenate([np.cos(th), np.sin(th)], axis=2)
    return tuple(jnp.asarray(t, dtype=F32).astype(BF16) for t in (w_ch, w1, w2))


def _fourier_kernel(f_ref, wch_ref, w1_ref, w2_ref, o_ref, x_scr, y_scr, o_scr):
    r = FFT_RADIX
    n = FOURIER_GROUP_DIM
    z = jnp.dot(f_ref[...], wch_ref[...], preferred_element_type=F32)
    x_scr[0:r] = z[:, :n].astype(BF16).reshape(r, r, n)
    x_scr[r:2 * r] = z[:, n:].astype(BF16).reshape(r, r, n)
    y = lax.dot_general(w1_ref[...], x_scr[...], (((1,), (0,)), ((), ())),
                        preferred_element_type=F32)
    y_scr[...] = y.astype(BF16)

    def body(c, carry):
        slab = y_scr[pl.ds(2 * c, 2)].reshape(2 * r, n)
        res = jnp.dot(w2_ref[c], slab, preferred_element_type=F32)
        for s in range(n // LANES):
            o_scr[s, pl.ds(c, r, stride=r), :] = res[:, s * LANES:(s + 1) * LANES]
        return carry

    lax.fori_loop(0, r, body, 0, unroll=FFT_STAGE2_UNROLL)
    for s in range(n // LANES):
        o_ref[:, s * LANES:(s + 1) * LANES] = o_scr[s].astype(BF16)


def _fourier(f):
    t_tokens = f.shape[1]
    n_b = t_tokens // SEQ
    r = FFT_RADIX
    n = FOURIER_GROUP_DIM
    w_ch, w1, w2 = _fourier_tables()
    blk = pl.BlockSpec((None, SEQ, n), lambda g, b: (g, b, 0))
    return pl.pallas_call(
        _fourier_kernel,
        grid=(N_FOURIER_GROUPS, n_b),
        in_specs=[
            blk,
            pl.BlockSpec((n, 2 * n), lambda g, b: (0, 0)),
            pl.BlockSpec((2 * r, 2 * r), lambda g, b: (0, 0)),
            pl.BlockSpec((r, r, 2 * r), lambda g, b: (0, 0, 0)),
        ],
        out_specs=blk,
        out_shape=jax.ShapeDtypeStruct((N_FOURIER_GROUPS, t_tokens, n), BF16),
        scratch_shapes=[
            pltpu.VMEM((2 * r, r, n), BF16),
            pltpu.VMEM((2 * r, r, n), BF16),
            pltpu.VMEM((n // LANES, SEQ, LANES), F32),
        ],
        compiler_params=_cparams(("arbitrary", "arbitrary")),
        name="fourier",
    )(f, w_ch, w1, w2)


def _post_mix_kernel(attn_ref, four_ref, gate_ref, x_ref, mod_ref, gpost_ref, gpre_ref,
                     wao_ref, wfo_ref, wmo_ref, wr_ref, br_ref,
                     x1_ref, h2_ref, logit_ref):
    a = jnp.dot(attn_ref[...], wao_ref[...], preferred_element_type=F32)
    fo = jnp.dot(four_ref[0], wfo_ref[0], preferred_element_type=F32)
    for g in range(1, N_FOURIER_GROUPS):
        fo = fo + jnp.dot(four_ref[g], wfo_ref[g], preferred_element_type=F32)
    ga = gate_ref[:, :D_MODEL].astype(F32)
    gf = gate_ref[:, D_MODEL:].astype(F32)
    m = (ga * a + gf * fo).astype(BF16)
    mixed = jnp.dot(m, wmo_ref[...], preferred_element_type=F32)
    ms = jnp.mean(mixed * mixed, axis=-1, keepdims=True)
    y = mixed * lax.rsqrt(ms + NORM_EPS) * gpost_ref[...]
    x1 = x_ref[...] + mod_ref[2:3, :] * y
    x1_ref[...] = x1
    ms1 = jnp.mean(x1 * x1, axis=-1, keepdims=True)
    h2 = (x1 * lax.rsqrt(ms1 + NORM_EPS) * gpre_ref[...]) * (1.0 + mod_ref[4:5, :]) + mod_ref[3:4, :]
    h2_ref[...] = h2
    logit_ref[...] = jnp.dot(h2.astype(BF16), wr_ref[...], preferred_element_type=F32) + br_ref[...]


def _post_mix(attn, four, gates, x2, mod3, g_post, g_pre, wao, wfo4, wmo, w_router, b_router):
    t_tokens = x2.shape[0]
    tm = POST_TM
    tiles_per_seq = SEQ // tm
    n = FOURIER_GROUP_DIM
    const2 = lambda i: (0, 0)
    resident = dict(pipeline_mode=pl.Buffered(1))
    return pl.pallas_call(
        _post_mix_kernel,
        grid=(t_tokens // tm,),
        in_specs=[
            pl.BlockSpec((tm, ATTN_Q_DIM), lambda i: (i, 0)),
            pl.BlockSpec((N_FOURIER_GROUPS, tm, n), lambda i: (0, i, 0)),
            pl.BlockSpec((tm, 2 * D_MODEL), lambda i: (i, 0)),
            pl.BlockSpec((tm, D_MODEL), lambda i: (i, 0)),
            pl.BlockSpec((None, 6, D_MODEL), lambda i: (i // tiles_per_seq, 0, 0)),
            pl.BlockSpec((1, D_MODEL), const2),
            pl.BlockSpec((1, D_MODEL), const2),
            pl.BlockSpec((ATTN_Q_DIM, D_MODEL), const2, **resident),
            pl.BlockSpec((N_FOURIER_GROUPS, n, D_MODEL), lambda i: (0, 0, 0), **resident),
            pl.BlockSpec((D_MODEL, D_MODEL), const2, **resident),
            pl.BlockSpec((D_MODEL, LANES), const2, **resident),
            pl.BlockSpec((1, LANES), const2),
        ],
        out_specs=[
            pl.BlockSpec((tm, D_MODEL), lambda i: (i, 0)),
            pl.BlockSpec((tm, D_MODEL), lambda i: (i, 0)),
            pl.BlockSpec((tm, LANES), lambda i: (i, 0)),
        ],
        out_shape=[
            jax.ShapeDtypeStruct((t_tokens, D_MODEL), F32),
            jax.ShapeDtypeStruct((t_tokens, D_MODEL), F32),
            jax.ShapeDtypeStruct((t_tokens, LANES), F32),
        ],
        compiler_params=_cparams(("arbitrary",)),
        name="post_mix",
    )(attn, four, gates, x2, mod3, g_post.reshape(1, D_MODEL), g_pre.reshape(1, D_MODEL),
      wao, wfo4, wmo, w_router, b_router)


def _first_argmax(vals, lane):
    mx = jnp.max(vals, axis=-1, keepdims=True)
    idx = jnp.min(jnp.where(vals == mx, lane, float(LANES)), axis=-1, keepdims=True)
    return mx, idx


def _route_kernel(logit_ref, tri_ref, out_ref, cnt_ref, carry_scr, start_scr):
    phase = pl.program_id(0)
    i = pl.program_id(1)

    @pl.when((phase == 0) & (i == 0))
    def _():
        carry_scr[...] = jnp.zeros_like(carry_scr)
        start_scr[...] = jnp.zeros_like(start_scr)

    @pl.when((phase == 1) & (i == 0))
    def _():
        counts = carry_scr[...]
        lane8 = lax.broadcasted_iota(jnp.int32, counts.shape, 1)
        incl = counts
        shift = 1
        while shift < N_EXPERTS:
            incl = incl + jnp.where(lane8 >= N_GROUPS + shift, pltpu.roll(incl, shift, 1), 0.0)
            shift *= 2
        start_scr[...] = incl - counts
        carry_scr[...] = jnp.zeros_like(carry_scr)

    lg = logit_ref[...]
    lane = lax.broadcasted_iota(jnp.int32, lg.shape, 1).astype(F32)
    neg = jnp.float32(-jnp.inf)
    gl = jnp.where(lane < N_GROUPS, lg, neg)
    gmax, gidx = _first_argmax(gl, lane)
    g_w = 1.0 / jnp.sum(jnp.exp(gl - gmax), axis=-1, keepdims=True)
    lo = N_GROUPS + EXPERTS_PER_GROUP * gidx
    el = jnp.where((lane >= lo) & (lane < lo + EXPERTS_PER_GROUP), lg, neg)
    m1, i1 = _first_argmax(el, lane)
    el2 = jnp.where(lane == i1, neg, el)
    m2, i2 = _first_argmax(el2, lane)
    p2 = jnp.exp(m2 - m1)
    w1 = g_w / (1.0 + p2)
    w2 = g_w * p2 / (1.0 + p2)
    oh = jnp.where((lane == i1) | (lane == i2), 1.0, 0.0)
    before = (jnp.dot(tri_ref[...], oh.astype(BF16), preferred_element_type=F32)
              + carry_scr[0:1, :] + start_scr[0:1, :])
    pos1 = jnp.sum(jnp.where(lane == i1, before, 0.0), axis=-1, keepdims=True)
    pos2 = jnp.sum(jnp.where(lane == i2, before, 0.0), axis=-1, keepdims=True)
    carry_scr[...] = carry_scr[...] + jnp.sum(oh, axis=0, keepdims=True)
    out = jnp.where(lane == 0, i1 - N_GROUPS, 0.0)
    out = jnp.where(lane == 1, i2 - N_GROUPS, out)
    out = jnp.where(lane == 2, pos1, out)
    out = jnp.where(lane == 3, pos2, out)
    out = jnp.where(lane == 4, w1, out)
    out = jnp.where(lane == 5, w2, out)
    out_ref[...] = out
    cnt_ref[...] = carry_scr[...]


def _route(logits):
    t_tokens = logits.shape[0]
    tb = ROUTE_TB
    tri = jnp.asarray(np.tril(np.ones((tb, tb), np.float32), -1), dtype=BF16)
    return pl.pallas_call(
        _route_kernel,
        grid=(2, t_tokens // tb),
        in_specs=[
            pl.BlockSpec((tb, LANES), lambda p, i: (i, 0)),
            pl.BlockSpec((tb, tb), lambda p, i: (0, 0)),
        ],
        out_specs=[
            pl.BlockSpec((tb, LANES), lambda p, i: (i * p, 0)),
            pl.BlockSpec((8, LANES), lambda p, i: (0, 0)),
        ],
        out_shape=[
            jax.ShapeDtypeStruct((t_tokens, LANES), F32),
            jax.ShapeDtypeStruct((8, LANES), F32),
        ],
        scratch_shapes=[pltpu.VMEM((8, LANES), F32), pltpu.VMEM((8, LANES), F32)],
        compiler_params=_cparams(("arbitrary", "arbitrary")),
        name="route",
    )(logits, tri)


def _row_gather(src_hbm, src_row, j, dst, sem):
    return pltpu.make_async_copy(src_hbm.at[pl.ds(src_row, 1), :], dst.at[pl.ds(j, 1), :], sem)


def _start_row_gathers(src_hbm, idx_ref, dst, sem, n_rows):
    def body(jj, carry):
        for u in range(GATHER_UNROLL):
            j = jj * GATHER_UNROLL + u
            _row_gather(src_hbm, idx_ref[0, 0, j], j, dst, sem).start(priority=u % 2)
        return carry

    lax.fori_loop(0, n_rows // GATHER_UNROLL, body, 0)


def _wait_row_gathers(src_hbm, dst, sem, n_rows):
    def body(jj, carry):
        for u in range(GATHER_UNROLL):
            _row_gather(src_hbm, 0, jj * GATHER_UNROLL + u, dst, sem).wait()
        return carry

    lax.fori_loop(0, n_rows // GATHER_UNROLL, body, 0)


def _row_scatter(src, j, dst_hbm, dst_row, sem):
    return pltpu.make_async_copy(src.at[pl.ds(j, 1), :], dst_hbm.at[pl.ds(dst_row, 1), :], sem)


def _dispatch_kernel(p1_ref, p2_ref, h_ref, xs_hbm, buf, sem):
    i = pl.program_id(0)
    slot = i % 2
    src = buf.at[slot]
    src[...] = h_ref[...]

    def start(jj, carry):
        for u in range(GATHER_UNROLL):
            j = jj * GATHER_UNROLL + u
            _row_scatter(src, j, xs_hbm, p1_ref[0, 0, j], sem.at[slot]).start(priority=0)
            _row_scatter(src, j, xs_hbm, p2_ref[0, 0, j], sem.at[slot]).start(priority=1)
        return carry

    lax.fori_loop(0, MOE_TB // GATHER_UNROLL, start, 0)

    def wait_block(s):
        def body(jj, carry):
            for u in range(GATHER_UNROLL):
                j = jj * GATHER_UNROLL + u
                _row_scatter(buf.at[s], j, xs_hbm, 0, sem.at[s]).wait()
                _row_scatter(buf.at[s], j, xs_hbm, 0, sem.at[s]).wait()
            return carry

        lax.fori_loop(0, MOE_TB // GATHER_UNROLL, body, 0)

    @pl.when(i > 0)
    def _():
        wait_block(1 - slot)

    @pl.when(i + 1 == pl.num_programs(0))
    def _():
        wait_block(slot)


def _dispatch(pos1, pos2, h2):
    t_tokens = h2.shape[0]
    nb = t_tokens // MOE_TB
    smem_blk = pl.BlockSpec((1, 1, MOE_TB), lambda i: (i, 0, 0), memory_space=pltpu.SMEM)
    return pl.pallas_call(
        _dispatch_kernel,
        grid=(nb,),
        in_specs=[smem_blk, smem_blk, pl.BlockSpec((MOE_TB, D_MODEL), lambda i: (i, 0))],
        out_specs=pl.BlockSpec(memory_space=pl.ANY),
        out_shape=jax.ShapeDtypeStruct((2 * t_tokens, D_MODEL), F32),
        scratch_shapes=[pltpu.VMEM((2, MOE_TB, D_MODEL), F32), pltpu.SemaphoreType.DMA((2,))],
        compiler_params=_cparams(("arbitrary",)),
        name="dispatch",
    )(pos1.reshape(nb, 1, MOE_TB), pos2.reshape(nb, 1, MOE_TB), h2)


def _expert_kernel(tile_ref, exp_ref, start_ref, end_ref, nitem_ref,
                   xs_ref, wg_ref, wu_ref, wd_ref, ys_ref, wg_scr, wu_scr, wd_scr):
    k = pl.program_id(0)
    tile = tile_ref[k]
    e = exp_ref[k]
    prev = jnp.maximum(k - 1, 0)
    first_visit = (k == 0) | (tile != tile_ref[prev])
    new_expert = (k == 0) | (e != exp_ref[prev])

    @pl.when(new_expert)
    def _():
        wg_scr[...] = wg_ref[...].astype(BF16)
        wu_scr[...] = wu_ref[...].astype(BF16)
        wd_scr[...] = wd_ref[...].astype(BF16)

    @pl.when(first_visit)
    def _():
        ys_ref[...] = jnp.zeros_like(ys_ref)

    @pl.when(k < nitem_ref[0])
    def _():
        x = xs_ref[...].astype(BF16)
        g = jnp.dot(x, wg_scr[...], preferred_element_type=F32)
        u = jnp.dot(x, wu_scr[...], preferred_element_type=F32)
        hmid = (g * jax.nn.sigmoid(g) * u).astype(BF16)
        y = jnp.dot(hmid, wd_scr[...], preferred_element_type=F32)
        row = tile * MOE_TM + lax.broadcasted_iota(jnp.int32, (MOE_TM, 1), 0)
        valid = (row >= start_ref[e]) & (row < end_ref[e])
        ys_ref[...] += jnp.where(valid, y, 0.0)


def _experts(item_tile, item_exp, starts, ends, n_items, xs, wg, wu, wd):
    n_rows = xs.shape[0]
    n_tiles = n_rows // MOE_TM
    max_items = n_tiles + N_EXPERTS - 1
    grid_spec = pltpu.PrefetchScalarGridSpec(
        num_scalar_prefetch=5,
        grid=(max_items,),
        in_specs=[
            pl.BlockSpec((MOE_TM, D_MODEL), lambda k, t, e, s, en, n: (t[k], 0)),
            pl.BlockSpec((None, D_MODEL, EXPERT_FF), lambda k, t, e, s, en, n: (e[k], 0, 0)),
            pl.BlockSpec((None, D_MODEL, EXPERT_FF), lambda k, t, e, s, en, n: (e[k], 0, 0)),
            pl.BlockSpec((None, EXPERT_FF, D_MODEL), lambda k, t, e, s, en, n: (e[k], 0, 0)),
        ],
        out_specs=pl.BlockSpec((MOE_TM, D_MODEL), lambda k, t, e, s, en, n: (t[k], 0)),
        scratch_shapes=[
            pltpu.VMEM((D_MODEL, EXPERT_FF), BF16),
            pltpu.VMEM((D_MODEL, EXPERT_FF), BF16),
            pltpu.VMEM((EXPERT_FF, D_MODEL), BF16),
        ],
    )
    return pl.pallas_call(
        _expert_kernel,
        grid_spec=grid_spec,
        out_shape=jax.ShapeDtypeStruct((n_rows, D_MODEL), F32),
        compiler_params=_cparams(("arbitrary",)),
        name="experts",
    )(item_tile, item_exp, starts, ends, n_items, xs, wg, wu, wd)


def _expert_items(counts):
    ends = jnp.cumsum(counts)
    starts = ends - counts
    first_tile = starts // MOE_TM
    last_tile = (ends - 1) // MOE_TM
    n_it = jnp.where(counts > 0, last_tile - first_tile + 1, 0)
    it_end = jnp.cumsum(n_it)
    it_start = it_end - n_it
    total = it_end[-1]
    return starts, ends, first_tile, it_start, it_end, total


def _combine_kernel(p1_ref, p2_ref, p1_next_ref, p2_next_ref, ys_hbm, route_ref, x1_ref, mod_ref, g_ref,
                    o_ref, y_scr, sem):
    i = pl.program_id(0)
    slot = i % 2

    def start_block(pa_ref, pb_ref, s):
        _start_row_gathers(ys_hbm, pa_ref, y_scr.at[s, 0], sem.at[s], MOE_TB)
        _start_row_gathers(ys_hbm, pb_ref, y_scr.at[s, 1], sem.at[s], MOE_TB)

    @pl.when(i == 0)
    def _():
        start_block(p1_ref, p2_ref, 0)

    @pl.when(i + 1 < pl.num_programs(0))
    def _():
        start_block(p1_next_ref, p2_next_ref, 1 - slot)

    _wait_row_gathers(ys_hbm, y_scr.at[slot, 0], sem.at[slot], MOE_TB)
    _wait_row_gathers(ys_hbm, y_scr.at[slot, 1], sem.at[slot], MOE_TB)
    w1 = route_ref[:, 4:5]
    w2 = route_ref[:, 5:6]
    ffn = w1 * y_scr[slot, 0] + w2 * y_scr[slot, 1]
    ms = jnp.mean(ffn * ffn, axis=-1, keepdims=True)
    y = ffn * lax.rsqrt(ms + NORM_EPS) * g_ref[...]
    o_ref[...] = x1_ref[...] + mod_ref[5:6, :] * y


def _combine(pos1, pos2, ys, route, x1, mod3, g_post):
    t_tokens = x1.shape[0]
    tb = MOE_TB
    nb = t_tokens // tb
    tiles_per_seq = SEQ // tb
    smem_blk = pl.BlockSpec((1, 1, tb), lambda i: (i, 0, 0), memory_space=pltpu.SMEM)
    smem_next = pl.BlockSpec((1, 1, tb), lambda i: (jnp.minimum(i + 1, nb - 1), 0, 0), memory_space=pltpu.SMEM)
    p1 = pos1.reshape(nb, 1, tb)
    p2 = pos2.reshape(nb, 1, tb)
    return pl.pallas_call(
        _combine_kernel,
        grid=(nb,),
        in_specs=[
            smem_blk, smem_blk, smem_next, smem_next,
            pl.BlockSpec(memory_space=pl.ANY),
            pl.BlockSpec((tb, LANES), lambda i: (i, 0)),
            pl.BlockSpec((tb, D_MODEL), lambda i: (i, 0)),
            pl.BlockSpec((None, 6, D_MODEL), lambda i: (i // tiles_per_seq, 0, 0)),
            pl.BlockSpec((1, D_MODEL), lambda i: (0, 0)),
        ],
        out_specs=pl.BlockSpec((tb, D_MODEL), lambda i: (i, 0)),
        out_shape=jax.ShapeDtypeStruct((t_tokens, D_MODEL), F32),
        scratch_shapes=[
            pltpu.VMEM((2, 2, tb, D_MODEL), F32),
            pltpu.SemaphoreType.DMA((2,)),
        ],
        compiler_params=_cparams(("arbitrary",)),
        name="combine",
    )(p1, p2, p1, p2, ys, route, x1, mod3, g_post.reshape(1, D_MODEL))


def _rope_tables():
    rows = SEQ // GRID_W
    row = jnp.repeat(jnp.arange(rows, dtype=F32), GRID_W)
    col = jnp.tile(jnp.arange(GRID_W, dtype=F32), rows)
    n_freq = HEAD_DIM // 4
    inv = ROPE_THETA ** (-jnp.arange(n_freq, dtype=F32) / n_freq)
    ang_r = row[:, None] * inv
    ang_c = col[:, None] * inv
    ang = jnp.concatenate([ang_r, ang_r, ang_c, ang_c], axis=-1)
    cos, sin = jnp.cos(ang), jnp.sin(ang)
    first = (jnp.arange(HEAD_DIM) % (2 * n_freq)) < n_freq
    sin_a = jnp.where(first, -sin, 0.0)
    sin_b = jnp.where(first, 0.0, sin)
    return cos, sin_a, sin_b


def kernel(x, c, w_ada, b_ada, g_pre_mix, g_post_mix, w_in, q_norm_g, k_norm_g, w_attn_out, w_fourier_out,
           w_mix_out, g_pre_ffn, g_post_ffn, w_group_router, b_group_router, w_expert_router, b_expert_router,
           w_exp_gate, w_exp_up, w_exp_down):
    n_b, seq, d = x.shape
    assert seq == SEQ and d == D_MODEL
    t_tokens = n_b * seq
    x2 = x.reshape(t_tokens, d)

    c_pad = jnp.zeros((8, d), F32).at[:n_b].set(c)
    mod = _ada(c_pad, w_ada, b_ada)[:n_b]
    mod3 = mod.reshape(n_b, 6, d)

    cos, sin_a, sin_b = _rope_tables()
    q, k, vt, f, gates = _in_proj(x2, mod3, g_pre_mix, _regroup_w_in(w_in), cos, sin_a, sin_b, q_norm_g, k_norm_g)
    attn = _attention(q, k, vt)
    four = _fourier(f)

    n_r = N_GROUPS + N_EXPERTS
    w_router = jnp.zeros((d, LANES), F32).at[:, :N_GROUPS].set(w_group_router).at[:, N_GROUPS:n_r].set(w_expert_router)
    b_router = jnp.zeros((1, LANES), F32).at[0, :N_GROUPS].set(b_group_router).at[0, N_GROUPS:n_r].set(b_expert_router)
    x1, h2, logits = _post_mix(
        attn, four, gates, x2, mod3, g_post_mix, g_pre_ffn,
        w_attn_out.astype(BF16),
        w_fourier_out.astype(BF16).reshape(N_FOURIER_GROUPS, FOURIER_GROUP_DIM, d),
        w_mix_out.astype(BF16), w_router.astype(BF16), b_router)

    route, cnt = _route(logits)
    counts = cnt[0, N_GROUPS:n_r].astype(jnp.int32)
    starts, ends, first_tile, it_start, it_end, total = _expert_items(counts)
    pos1 = route[:, 2].astype(jnp.int32)
    pos2 = route[:, 3].astype(jnp.int32)

    xs = _dispatch(pos1, pos2, h2)

    n_tiles = 2 * t_tokens // MOE_TM
    max_items = n_tiles + N_EXPERTS - 1
    kk = jnp.minimum(jnp.arange(max_items, dtype=jnp.int32), total - 1)
    item_exp = jnp.sum((it_end[None, :] <= kk[:, None]).astype(jnp.int32), axis=1)
    item_tile = first_tile[item_exp] + kk - it_start[item_exp]
    ys = _experts(item_tile.astype(jnp.int32), item_exp.astype(jnp.int32), starts.astype(jnp.int32),
                  ends.astype(jnp.int32), total.reshape(1).astype(jnp.int32), xs, w_exp_gate, w_exp_up, w_exp_down)

    out = _combine(pos1, pos2, ys, route, x1, mod3, g_post_ffn)
    return out.reshape(n_b, seq, d)
```
